```python
import jax
import jax.numpy as jnp
from jax import lax
import numpy as np

D_MODEL = 1024
BATCH = 4
SEQ = 8192
DEPTH = 1
DEC_BATCH = 32
DEC_SEQ = 4
PAST_LEN = 16384
PAGE_SIZE = 128

HEAD_DIM = 64
W_ATTN = D_MODEL // 2
N_HEADS_A = W_ATTN // HEAD_DIM
W_RG = D_MODEL - W_ATTN
N_RG_BLOCKS = 8
RG_BLOCK = W_RG // N_RG_BLOCKS
CONV_WIDTH = 4
RG_C = 8.0
Q_BLOCK = 128
N_GROUPS = 4
EXPERTS_PER_GROUP = 8
N_EXPERTS = N_GROUPS * EXPERTS_PER_GROUP
TOP_K = 2
D_EXPERT = D_MODEL // 2
EXPERT_BLOCK = 128
RMS_EPS = 1e-6
D_IN = 3 * W_ATTN + 2 * W_RG
SB_BIAS_NEAR = -2.0
SB_BIAS_FAR = -9.0

kernel_name = 'hymba_stickbreak_rglru_hmoe_step'


def rms_norm(x, gain):
    xf = x.astype(jnp.float32)
    y = xf * lax.rsqrt(jnp.mean(xf * xf, axis=-1, keepdims=True) + RMS_EPS)
    return (y * gain.astype(jnp.float32)).astype(x.dtype)


def stick_breaking_block(q, k, v, bias, q_pos, k_pos):
    z = jnp.einsum('bqhd,bkhd->bhqk', q.astype(jnp.float32), k.astype(jnp.float32)) * (HEAD_DIM ** -0.5)
    z = z + bias.astype(jnp.float32)[None, :, None, None]
    causal = k_pos[None, :] < q_pos[:, None]
    log_1m_beta = jnp.where(causal, jax.nn.log_sigmoid(-z), 0.0)
    tail = lax.cumsum(log_1m_beta, axis=3, reverse=True) - log_1m_beta
    w = jnp.where(causal, jnp.exp(jax.nn.log_sigmoid(z) + tail), 0.0)
    return jnp.einsum('bhqk,bkhd->bqhd', w.astype(v.dtype), v)


def stick_breaking_prompt(q, k, v, bias):
    B, T, H, Dh = q.shape
    nb = T // Q_BLOCK
    q_blocks = q.reshape(B, nb, Q_BLOCK, H, Dh).transpose(1, 0, 2, 3, 4)
    q_pos = jnp.arange(T, dtype=jnp.int32).reshape(nb, Q_BLOCK)
    k_pos = jnp.arange(T, dtype=jnp.int32)
    out = lax.map(lambda a: stick_breaking_block(a[0], k, v, bias, a[1], k_pos), (q_blocks, q_pos))
    return out.transpose(1, 0, 2, 3, 4).reshape(B, T, H, Dh)


def causal_conv(xr, buf, w, b):
    T = xr.shape[1]
    xp = jnp.concatenate([buf.astype(xr.dtype), xr], axis=1)
    out = b
    for j in range(CONV_WIDTH):
        out = out + w[j] * xp[:, j:j + T]
    return out, xp[:, T:]


def rg_lru(xc, h0, wa, ba, wx, bx, lam):
    B, T, C = xc.shape
    xf = xc.astype(jnp.float32)
    xh = xf.reshape(B, T, N_RG_BLOCKS, RG_BLOCK)
    r = jax.nn.sigmoid(jnp.einsum('btnc,ncd->btnd', xh, wa.astype(jnp.float32)).reshape(B, T, C) + ba)
    i = jax.nn.sigmoid(jnp.einsum('btnc,ncd->btnd', xh, wx.astype(jnp.float32)).reshape(B, T, C) + bx)
    log_a = -RG_C * r * jax.nn.softplus(-lam.astype(jnp.float32))
    a = jnp.exp(log_a)
    b_in = jnp.sqrt(-jnp.expm1(2.0 * log_a)) * (i * xf)
    b_in = b_in.at[:, 0].add(a[:, 0] * h0.astype(jnp.float32))

    def combine(left, right):
        a1, b1 = left
        a2, b2 = right
        return a1 * a2, a2 * b1 + b2

    _, h = lax.associative_scan(combine, (a, b_in), axis=1)
    return h.astype(xc.dtype), h[:, -1].astype(xc.dtype)


def routed_experts(x, expert_idx, gate, w1, w3, w2):
    n_tok, d = x.shape
    n_slot = expert_idx.shape[0]
    slot_tok = jnp.arange(n_slot, dtype=jnp.int32) // TOP_K
    order = jnp.argsort(expert_idx)
    e_sorted = expert_idx[order]
    counts = jax.ops.segment_sum(jnp.ones_like(expert_idx), expert_idx, num_segments=N_EXPERTS)
    padded = (counts + EXPERT_BLOCK - 1) // EXPERT_BLOCK * EXPERT_BLOCK
    pad_end = jnp.cumsum(padded)
    pad_start = pad_end - padded
    start = jnp.cumsum(counts) - counts
    dest = pad_start[e_sorted] + (jnp.arange(n_slot, dtype=jnp.int32) - start[e_sorted])
    n_blocks = (n_slot + N_EXPERTS * (EXPERT_BLOCK - 1) + EXPERT_BLOCK - 1) // EXPERT_BLOCK
    n_rows = n_blocks * EXPERT_BLOCK
    row_tok = jnp.full((n_rows,), n_tok, jnp.int32).at[dest].set(slot_tok[order])
    row_gate = jnp.zeros((n_rows,), jnp.float32).at[dest].set(gate[order].astype(jnp.float32))
    block_start = jnp.arange(n_blocks, dtype=jnp.int32) * EXPERT_BLOCK
    block_expert = jnp.minimum(jnp.searchsorted(pad_end, block_start, side='right'), N_EXPERTS - 1)
    x_pad = jnp.concatenate([x, jnp.zeros((1, d), x.dtype)], axis=0)

    def expert_block(args):
        toks, e = args
        xb = x_pad[toks]
        hb = jax.nn.silu(xb @ w1[e]) * (xb @ w3[e])
        return hb @ w2[e]

    out = lax.map(expert_block, (row_tok.reshape(n_blocks, EXPERT_BLOCK), block_expert))
    out = out.reshape(n_rows, d) * row_gate[:, None].astype(x.dtype)
    return jax.ops.segment_sum(out, row_tok, num_segments=n_tok + 1)[:n_tok]


def hier_moe(x, w_group, b_group, w_router, b_router, w1, w3, w2):
    n_tok = x.shape[0]
    rows = jnp.arange(n_tok)
    xf = x.astype(jnp.float32)
    g_logits = xf @ w_group.astype(jnp.float32) + b_group.astype(jnp.float32)
    p_group = jax.nn.softmax(g_logits, axis=-1)
    grp = jnp.argmax(g_logits, axis=-1).astype(jnp.int32)
    p_sel = p_group[rows, grp][:, None]
    e_logits = (xf @ w_router.astype(jnp.float32) + b_router.astype(jnp.float32)).reshape(n_tok, N_GROUPS, EXPERTS_PER_GROUP)
    e_in = e_logits[rows, grp]
    top_val, top_idx = lax.top_k(e_in, TOP_K)
    gates = p_sel * jax.nn.softmax(top_val, axis=-1)
    experts = grp[:, None] * EXPERTS_PER_GROUP + top_idx.astype(jnp.int32)
    return routed_experts(x, experts.reshape(-1), gates.reshape(-1), w1, w3, w2)


def decoder_layer(x, c, conv_buf, h0, attend, p):
    B, T, D = x.shape
    mod = (jax.nn.silu(c) @ p['w_ada'] + p['b_ada'])[:, None, :]
    shift1, scale1, gate1, shift2, scale2, gate2 = jnp.split(mod, 6, axis=-1)
    xn = rms_norm(x, p['norm_mix']) * (1 + scale1) + shift1
    proj = xn @ p['w_in']
    q, k, v, xr, yb = jnp.split(proj, [W_ATTN, 2 * W_ATTN, 3 * W_ATTN, 3 * W_ATTN + W_RG], axis=-1)
    q = q.reshape(B, T, N_HEADS_A, HEAD_DIM)
    k = k.reshape(B, T, N_HEADS_A, HEAD_DIM)
    v = v.reshape(B, T, N_HEADS_A, HEAD_DIM)
    attn = rms_norm(attend(q, k, v, p['sb_bias']), p['norm_attn']).reshape(B, T, W_ATTN)
    xc, conv_new = causal_conv(xr, conv_buf, p['conv_w'], p['conv_b'])
    h, h_last = rg_lru(xc, h0, p['rg_wa'], p['rg_ba'], p['rg_wx'], p['rg_bx'], p['rg_lambda'])
    rg = rms_norm(h * jax.nn.gelu(yb), p['norm_rg'])
    x = x + gate1 * (jnp.concatenate([attn, rg], axis=-1) @ p['w_out'])
    xn2 = rms_norm(x, p['norm_ffn']) * (1 + scale2) + shift2
    ffn = hier_moe(xn2.reshape(B * T, D), p['w_group'], p['b_group'], p['w_router'], p['b_router'],
                   p['w1'], p['w3'], p['w2']).reshape(B, T, D)
    x = x + gate2 * ffn
    return x, k, v, conv_new, h_last


def setup_inputs(seed: int = 0) -> dict:
    key = jax.random.key(seed)
    ks = jax.random.split(key, 40)
    n_pages = PAST_LEN // PAGE_SIZE
    n_used = DEC_BATCH * n_pages
    n_phys = n_used + n_used // 4
    f32 = jnp.float32

    def nrm(k, shape, scale=1.0):
        return jax.random.normal(k, shape, f32) * scale

    u = jax.random.uniform(ks[0], (DEPTH, W_RG), f32, minval=0.9, maxval=0.999)
    s = u ** (1.0 / RG_C)
    rg_lambda = jnp.log(s) - jnp.log1p(-s)
    perm = jax.random.permutation(ks[1], n_phys)[:n_used]
    page_table = perm.reshape(DEC_BATCH, n_pages).astype(jnp.int32)
    sb_bias = jnp.linspace(SB_BIAS_NEAR, SB_BIAS_FAR, N_HEADS_A, dtype=f32)[None, :] + nrm(ks[32], (DEPTH, N_HEADS_A), 0.1)
    return {
        'x_prompt': nrm(ks[2], (BATCH, SEQ, D_MODEL)),
        'x_sample': nrm(ks[3], (DEC_BATCH, DEC_SEQ, D_MODEL)),
        'cache_k': nrm(ks[4], (DEPTH, n_phys, PAGE_SIZE, N_HEADS_A, HEAD_DIM)),
        'cache_v': nrm(ks[5], (DEPTH, n_phys, PAGE_SIZE, N_HEADS_A, HEAD_DIM)),
        'state_conv': nrm(ks[6], (DEPTH, DEC_BATCH, CONV_WIDTH - 1, W_RG)),
        'state_h': nrm(ks[7], (DEPTH, DEC_BATCH, W_RG), 0.5),
        'page_table': page_table,
        'c_prompt': nrm(ks[8], (BATCH, D_MODEL)),
        'c_sample': nrm(ks[9], (DEC_BATCH, D_MODEL)),
        'w_ada': nrm(ks[10], (DEPTH, D_MODEL, 6 * D_MODEL), D_MODEL ** -0.5),
        'b_ada': nrm(ks[11], (DEPTH, 6 * D_MODEL), 0.01),
        'norm_mix': 1.0 + nrm(ks[12], (DEPTH, D_MODEL), 0.01),
        'norm_ffn': 1.0 + nrm(ks[13], (DEPTH, D_MODEL), 0.01),
        'w_in': nrm(ks[14], (DEPTH, D_MODEL, D_IN), D_MODEL ** -0.5),
        'sb_bias': sb_bias,
        'norm_attn': 1.0 + nrm(ks[15], (DEPTH, N_HEADS_A, HEAD_DIM), 0.01),
        'conv_w': nrm(ks[16], (DEPTH, CONV_WIDTH, W_RG), CONV_WIDTH ** -0.5),
        'conv_b': nrm(ks[17], (DEPTH, W_RG), 0.01),
        'rg_wa': nrm(ks[18], (DEPTH, N_RG_BLOCKS, RG_BLOCK, RG_BLOCK), RG_BLOCK ** -0.5),
        'rg_ba': nrm(ks[19], (DEPTH, W_RG), 0.01),
        'rg_wx': nrm(ks[20], (DEPTH, N_RG_BLOCKS, RG_BLOCK, RG_BLOCK), RG_BLOCK ** -0.5),
        'rg_bx': nrm(ks[21], (DEPTH, W_RG), 0.01),
        'rg_lambda': rg_lambda,
        'norm_rg': 1.0 + nrm(ks[22], (DEPTH, W_RG), 0.01),
        'w_out': nrm(ks[23], (DEPTH, W_ATTN + W_RG, D_MODEL), (W_ATTN + W_RG) ** -0.5),
        'w_group': nrm(ks[24], (DEPTH, D_MODEL, N_GROUPS), D_MODEL ** -0.5),
        'b_group': nrm(ks[25], (DEPTH, N_GROUPS), 0.01),
        'w_router': nrm(ks[26], (DEPTH, D_MODEL, N_EXPERTS), D_MODEL ** -0.5),
        'b_router': nrm(ks[27], (DEPTH, N_EXPERTS), 0.01),
        'w1': nrm(ks[28], (DEPTH, N_EXPERTS, D_MODEL, D_EXPERT), D_MODEL ** -0.5),
        'w3': nrm(ks[29], (DEPTH, N_EXPERTS, D_MODEL, D_EXPERT), D_MODEL ** -0.5),
        'w2': nrm(ks[30], (DEPTH, N_EXPERTS, D_EXPERT, D_MODEL), D_EXPERT ** -0.5),
        'final_norm': 1.0 + nrm(ks[31], (D_MODEL,), 0.01),
    }


def reference(x_prompt, x_sample, cache_k, cache_v, state_conv, state_h, page_table,
              c_prompt, c_sample, w_ada, b_ada, norm_mix, norm_ffn, w_in, sb_bias, norm_attn,
              conv_w, conv_b, rg_wa, rg_ba, rg_wx, rg_bx, rg_lambda, norm_rg, w_out,
              w_group, b_group, w_router, b_router, w1, w3, w2, final_norm):
    n_seq_s = x_sample.shape[0]
    past_len = page_table.shape[1] * PAGE_SIZE
    xp, xs = x_prompt, x_sample
    kp_all, vp_all, convp_all, hp_all = [], [], [], []
    ks_all, vs_all, convs_all, hs_all = [], [], [], []
    for l in range(DEPTH):
        p = dict(w_ada=w_ada[l], b_ada=b_ada[l], norm_mix=norm_mix[l], norm_ffn=norm_ffn[l],
                 w_in=w_in[l], sb_bias=sb_bias[l], norm_attn=norm_attn[l], conv_w=conv_w[l],
                 conv_b=conv_b[l], rg_wa=rg_wa[l], rg_ba=rg_ba[l], rg_wx=rg_wx[l], rg_bx=rg_bx[l],
                 rg_lambda=rg_lambda[l], norm_rg=norm_rg[l], w_out=w_out[l],
                 w_group=w_group[l], b_group=b_group[l], w_router=w_router[l],
                 b_router=b_router[l], w1=w1[l], w3=w3[l], w2=w2[l])

        def attend_sample(q, k, v, bias, l=l):
            t_new = q.shape[1]
            k_past = cache_k[l][page_table].reshape(n_seq_s, past_len, N_HEADS_A, HEAD_DIM)
            v_past = cache_v[l][page_table].reshape(n_seq_s, past_len, N_HEADS_A, HEAD_DIM)
            keys = jnp.concatenate([k_past, k.astype(k_past.dtype)], axis=1)
            vals = jnp.concatenate([v_past, v.astype(v_past.dtype)], axis=1)
            q_pos = past_len + jnp.arange(t_new, dtype=jnp.int32)
            k_pos = jnp.arange(past_len + t_new, dtype=jnp.int32)
            return stick_breaking_block(q, keys, vals, bias, q_pos, k_pos)

        conv0 = jnp.zeros((xp.shape[0], CONV_WIDTH - 1, W_RG), xp.dtype)
        h0 = jnp.zeros((xp.shape[0], W_RG), jnp.float32)
        xp, kp, vp, convp, hp = decoder_layer(xp, c_prompt, conv0, h0, stick_breaking_prompt, p)
        xs, k_s, v_s, convs, hs = decoder_layer(xs, c_sample, state_conv[l], state_h[l], attend_sample, p)
        kp_all.append(kp)
        vp_all.append(vp)
        convp_all.append(convp)
        hp_all.append(hp)
        ks_all.append(k_s)
        vs_all.append(v_s)
        convs_all.append(convs)
        hs_all.append(hs)
    y_prompt = rms_norm(xp, final_norm)
    y_sample = rms_norm(xs, final_norm)
    k_prompt = jnp.stack(kp_all)
    v_prompt = jnp.stack(vp_all)
    conv_prompt = jnp.stack(convp_all)
    h_prompt = jnp.stack(hp_all)
    k_sample = jnp.stack(ks_all)
    v_sample = jnp.stack(vs_all)
    conv_sample = jnp.stack(convs_all)
    h_sample = jnp.stack(hs_all)
    return (y_prompt, y_sample, k_prompt, v_prompt, conv_prompt, h_prompt, k_sample, v_sample, conv_sample, h_sample)
```

```python
import functools

import jax
import jax.numpy as jnp
from jax import lax
from jax.experimental import pallas as pl
from jax.experimental.pallas import tpu as pltpu

HEAD_DIM = 64
N_RG_BLOCKS = 8
CONV_WIDTH = 4
RG_C = 8.0
N_GROUPS = 4
EXPERTS_PER_GROUP = 8
N_EXPERTS = N_GROUPS * EXPERTS_PER_GROUP
TOP_K = 2
RMS_EPS = 1e-6
PAGE_SIZE = 128

V7X_LANES = 128
V7X_SUBLANES = 8
VMEM_LIMIT = 48 * 1024 * 1024

MOE_BLOCK = 256
ATTN_TILE = 256
PAGES_PER_STEP = 16

_F32 = jnp.float32
_BF16 = jnp.bfloat16
_HIGHEST = lax.Precision.HIGHEST


def _dot(a, b):
    return jnp.dot(a.astype(_BF16), b.astype(_BF16), preferred_element_type=_F32)


def _dot_nt(a, b):
    return lax.dot_general(a.astype(_BF16), b.astype(_BF16), (((1,), (1,)), ((), ())),
                           preferred_element_type=_F32)


def _dot_f32(a, b):
    return jnp.dot(a, b, precision=_HIGHEST, preferred_element_type=_F32)


def _rms(x, gain):
    return x * lax.rsqrt(jnp.mean(x * x, axis=-1, keepdims=True) + RMS_EPS) * gain


def _softplus(z):
    return jnp.maximum(z, 0.0) + jnp.log1p(jnp.exp(-jnp.abs(z)))


def _div_pow2(x, n):
    assert n & (n - 1) == 0
    return lax.shift_right_logical(x, n.bit_length() - 1)


def _mod_pow2(x, n):
    assert n & (n - 1) == 0
    return lax.bitwise_and(x, n - 1)


def _params(*sem):
    return pltpu.CompilerParams(dimension_semantics=sem, vmem_limit_bytes=VMEM_LIMIT)


def _ada_kernel(c_ref, w_ref, b_ref, o_ref):
    c = c_ref[...]
    o_ref[...] = _dot(c * jax.nn.sigmoid(c), w_ref[...]) + b_ref[...]


def _ada(c, w_ada, b_ada):
    rows, d = c.shape
    n_chunks = w_ada.shape[1] // d
    return pl.pallas_call(
        _ada_kernel,
        out_shape=jax.ShapeDtypeStruct((rows, n_chunks * d), _F32),
        grid=(n_chunks,),
        in_specs=[pl.BlockSpec((rows, d), lambda j: (0, 0)),
                  pl.BlockSpec((d, d), lambda j: (0, j)),
                  pl.BlockSpec((1, d), lambda j: (0, j))],
        out_specs=pl.BlockSpec((rows, d), lambda j: (0, j)),
        compiler_params=_params("parallel"),
        name="ada",
    )(c, w_ada, b_ada.reshape(1, -1))


def _inproj_kernel(x_ref, scale_ref, shift_ref, g_ref, w_ref, q16_ref, k_ref, v_ref, xr_ref, yb_ref,
                   k16_ref, v16_ref, *, w_attn):
    xn = (_rms(x_ref[0], g_ref[...]) * (1.0 + scale_ref[0]) + shift_ref[0]).astype(_BF16)

    def col(j):
        return _dot(xn, w_ref[:, j * w_attn:(j + 1) * w_attn])

    q16_ref[0] = (col(0) * (HEAD_DIM ** -0.5)).astype(_BF16)
    k = col(1)
    v = col(2)
    k_ref[0] = k
    v_ref[0] = v
    k16_ref[0] = k.astype(_BF16)
    v16_ref[0] = v.astype(_BF16)
    xr_ref[0] = col(3)
    yb_ref[0] = col(4)


def _inproj(x, scale, shift, gain, w_in, *, tm):
    g, t, d = x.shape
    w_attn = w_in.shape[1] // 5
    mod_rows = scale.shape[1]
    mod_block = (1, 1, d) if mod_rows == 1 else (1, tm, d)
    mod_map = (lambda b, i: (b, 0, 0)) if mod_rows == 1 else (lambda b, i: (b, i, 0))
    out_block = pl.BlockSpec((1, tm, w_attn), lambda b, i: (b, i, 0))
    out_shape = [jax.ShapeDtypeStruct((g, t, w_attn), dt) for dt in (_BF16, _F32, _F32, _F32, _F32, _BF16, _BF16)]
    return pl.pallas_call(
        functools.partial(_inproj_kernel, w_attn=w_attn),
        out_shape=out_shape,
        grid=(g, t // tm),
        in_specs=[pl.BlockSpec((1, tm, d), lambda b, i: (b, i, 0)),
                  pl.BlockSpec(mod_block, mod_map),
                  pl.BlockSpec(mod_block, mod_map),
                  pl.BlockSpec((1, d), lambda b, i: (0, 0)),
                  pl.BlockSpec(w_in.shape, lambda b, i: (0, 0))],
        out_specs=[out_block] * len(out_shape),
        compiler_params=_params("parallel", "parallel"),
        name="inproj",
    )(x, scale, shift, gain.reshape(1, d), w_in)


def _head_pair_norm(acc, gain, head0):
    sq = acc * acc
    s0 = jnp.sum(jnp.where(head0, sq, 0.0), axis=-1, keepdims=True)
    s1 = jnp.sum(jnp.where(head0, 0.0, sq), axis=-1, keepdims=True)
    ms = jnp.where(head0, s0, s1) * (1.0 / HEAD_DIM)
    return acc * lax.rsqrt(ms + RMS_EPS) * gain


def _attn_prompt_kernel(bias_ref, q_ref, k_ref, v_ref, g_ref, o_ref, *, tile):
    hp = pl.program_id(1)
    qi = pl.program_id(2)
    lane = lax.broadcasted_iota(jnp.int32, (1, 2 * HEAD_DIM), 1)
    head0 = lane < HEAD_DIM
    q = q_ref[0]
    zero = jnp.zeros_like(q)
    q_heads = (jnp.where(head0, q, zero), jnp.where(head0, zero, q))
    biases = (bias_ref[2 * hp], bias_ref[2 * hp + 1])
    row = lax.broadcasted_iota(jnp.int32, (tile, tile), 0)
    col = lax.broadcasted_iota(jnp.int32, (tile, tile), 1)
    suffix = (row > col).astype(_BF16)
    causal = col < row

    def key_tile(kt, carry, masked):
        acc, c0, c1 = carry
        start = pl.multiple_of(kt * tile, tile)
        k = k_ref[0, pl.ds(start, tile), :]
        v = v_ref[0, pl.ds(start, tile), :]
        pv = []
        cs = []
        for qh, bias, c in zip(q_heads, biases, (c0, c1)):
            z = _dot_nt(qh, k) + bias
            log1m = -_softplus(z)
            if masked:
                log1m = jnp.where(causal, log1m, 0.0)
            tail = _dot(log1m, suffix) + c
            w = jnp.exp(z + log1m + tail)
            if masked:
                w = jnp.where(causal, w, 0.0)
            pv.append(_dot(w, v))
            cs.append(c + jnp.sum(log1m, axis=-1, keepdims=True))
        acc = acc + jnp.where(head0, pv[0], pv[1])
        return acc, cs[0], cs[1]

    init = (jnp.zeros((tile, 2 * HEAD_DIM), _F32), jnp.zeros((tile, 1), _F32), jnp.zeros((tile, 1), _F32))
    carry = key_tile(qi, init, True)
    carry = lax.fori_loop(0, qi, lambda j, cr: key_tile(qi - 1 - j, cr, False), carry)
    o_ref[0] = _head_pair_norm(carry[0], g_ref[0], head0)


def _attn_prompt(q16, k16, v16, sb_bias, norm_attn):
    b, t, w_attn = q16.shape
    pair = 2 * HEAD_DIM
    n_pairs = w_attn // pair
    tile = min(ATTN_TILE, t)
    return pl.pallas_call(
        functools.partial(_attn_prompt_kernel, tile=tile),
        out_shape=jax.ShapeDtypeStruct((b, t, w_attn), _F32),
        grid=(b, n_pairs, t // tile),
        in_specs=[pl.BlockSpec(memory_space=pltpu.SMEM),
                  pl.BlockSpec((1, tile, pair), lambda bi, hp, qi: (bi, qi, hp)),
                  pl.BlockSpec((1, t, pair), lambda bi, hp, qi: (bi, 0, hp)),
                  pl.BlockSpec((1, t, pair), lambda bi, hp, qi: (bi, 0, hp)),
                  pl.BlockSpec((1, 1, pair), lambda bi, hp, qi: (hp, 0, 0))],
        out_specs=pl.BlockSpec((1, tile, pair), lambda bi, hp, qi: (bi, qi, hp)),
        compiler_params=_params("parallel", "parallel", "parallel"),
        name="attn_prompt",
    )(sb_bias, q16, k16, v16, norm_attn.reshape(n_pairs, 1, pair))


def _attn_sample_kernel(pt_ref, q_ref, kn_ref, vn_ref, bias_ref, g_ref, *rest, n_heads, t_new, pages):
    k_refs = rest[:pages]
    v_refs = rest[pages:2 * pages]
    o_ref, qbd_ref, acc_ref, c_ref = rest[2 * pages:]
    step = pl.program_id(1)
    rows = n_heads * t_new
    width = n_heads * HEAD_DIM
    r_id = lax.broadcasted_iota(jnp.int32, (rows, width), 0)
    l_id = lax.broadcasted_iota(jnp.int32, (rows, width), 1)
    own_head = _div_pow2(l_id, HEAD_DIM) == _div_pow2(r_id, t_new)
    bias = bias_ref[...]

    def suffix_matrix(n):
        return (lax.broadcasted_iota(jnp.int32, (n, n), 0) > lax.broadcasted_iota(jnp.int32, (n, n), 1)).astype(_F32)

    suffix_page = suffix_matrix(PAGE_SIZE)

    def visit(k, v, visible, suffix):
        z = _dot_nt(qbd_ref[...], k) + bias
        log1m = -_softplus(z)
        if visible is not None:
            log1m = jnp.where(visible, log1m, 0.0)
        tail = _dot_f32(log1m, suffix) + c_ref[...]
        w = jnp.exp(z + log1m + tail)
        if visible is not None:
            w = jnp.where(visible, w, 0.0)
        acc_ref[...] += _dot(w, v)
        c_ref[...] += jnp.sum(log1m, axis=-1, keepdims=True)

    @pl.when(step == 0)
    def _():
        rep_r = lax.broadcasted_iota(jnp.int32, (rows, t_new), 0)
        rep_c = lax.broadcasted_iota(jnp.int32, (rows, t_new), 1)
        replicate = (_mod_pow2(rep_r, t_new) == rep_c).astype(_F32)
        q_rep = _dot(replicate, q_ref[0])
        qbd_ref[...] = jnp.where(own_head, q_rep, 0.0).astype(_BF16)
        acc_ref[...] = jnp.zeros_like(acc_ref)
        c_ref[...] = jnp.zeros_like(c_ref)
        n_pad = kn_ref.shape[1]
        kr = _mod_pow2(lax.broadcasted_iota(jnp.int32, (rows, n_pad), 0), t_new)
        kc = lax.broadcasted_iota(jnp.int32, (rows, n_pad), 1)
        visit(kn_ref[0], vn_ref[0], kc < kr, suffix_matrix(n_pad))

    for j in range(pages):
        visit(k_refs[j][0], v_refs[j][0], None, suffix_page)

    @pl.when(step == pl.num_programs(1) - 1)
    def _():
        am = jnp.where(own_head, acc_ref[...], 0.0)
        ms = jnp.sum(am * am, axis=-1, keepdims=True) * (1.0 / HEAD_DIM)
        an = am * lax.rsqrt(ms + RMS_EPS)
        col_r = lax.broadcasted_iota(jnp.int32, (t_new, rows), 0)
        col_c = lax.broadcasted_iota(jnp.int32, (t_new, rows), 1)
        collect = (_mod_pow2(col_c, t_new) == col_r).astype(_F32)
        o_ref[0] = _dot_f32(collect, an) * g_ref[...]


def _attn_sample(q, k_new, v_new, cache_k, cache_v, page_table, sb_bias, norm_attn):
    s, t_new, width = q.shape
    n_heads = width // HEAD_DIM
    n_pages = page_table.shape[1]
    pages = min(PAGES_PER_STEP, n_pages)
    steps = n_pages // pages
    n_phys = cache_k.shape[0]
    ck = cache_k.reshape(n_phys, PAGE_SIZE, width)
    cv = cache_v.reshape(n_phys, PAGE_SIZE, width)
    pad = V7X_SUBLANES - t_new
    kn = jnp.pad(k_new, ((0, 0), (0, pad), (0, 0)))
    vn = jnp.pad(v_new, ((0, 0), (0, pad), (0, 0)))
    bias_rows = jnp.repeat(sb_bias, t_new).reshape(n_heads * t_new, 1)
    rows = n_heads * t_new

    def page_spec(j):
        return pl.BlockSpec((1, PAGE_SIZE, width),
                            lambda b, st, pt: (pt[b, n_pages - 1 - (st * pages + j)], 0, 0))

    seq_spec = lambda n: pl.BlockSpec((1, n, width), lambda b, st, pt: (b, 0, 0))
    grid_spec = pltpu.PrefetchScalarGridSpec(
        num_scalar_prefetch=1,
        grid=(s, steps),
        in_specs=[seq_spec(t_new), seq_spec(V7X_SUBLANES), seq_spec(V7X_SUBLANES),
                  pl.BlockSpec((rows, 1), lambda b, st, pt: (0, 0)),
                  pl.BlockSpec((1, width), lambda b, st, pt: (0, 0))]
                 + [page_spec(j) for j in range(pages)] * 2,
        out_specs=seq_spec(t_new),
        scratch_shapes=[pltpu.VMEM((rows, width), _BF16), pltpu.VMEM((rows, width), _F32),
                        pltpu.VMEM((rows, 1), _F32)],
    )
    return pl.pallas_call(
        functools.partial(_attn_sample_kernel, n_heads=n_heads, t_new=t_new, pages=pages),
        out_shape=jax.ShapeDtypeStruct((s, t_new, width), _F32),
        grid_spec=grid_spec,
        compiler_params=_params("parallel", "arbitrary"),
        name="attn_sample",
    )(page_table, q, kn, vn, bias_rows, norm_attn.reshape(1, width),
      *([ck] * pages), *([cv] * pages))


def _scan_rows(a, b, h_prev):
    n = a.shape[0]
    row = lax.broadcasted_iota(jnp.int32, (n, 1), 0)
    if n <= V7X_SUBLANES:
        h = jnp.zeros_like(b)
        for t in range(n):
            h_prev = a[t:t + 1] * h_prev + b[t:t + 1]
            h = jnp.where(row == t, h_prev, h)
        return h
    d = 1
    while d < n:
        keep = row >= d
        a_prev = jnp.where(keep, pltpu.roll(a, d, 0), 1.0)
        b_prev = jnp.where(keep, pltpu.roll(b, d, 0), 0.0)
        b = b + a * b_prev
        a = a * a_prev
        d *= 2
    return b + a * h_prev


def _rglru_kernel(xr_ref, yb_ref, conv0_ref, h0_ref, cw_ref, cb_ref, wa_ref, ba_ref, wx_ref, bx_ref,
                  lam_ref, g_ref, rg_ref, conv_out_ref, h_out_ref, xbuf_ref, h_ref, *, tt):
    i = pl.program_id(1)
    hist = CONV_WIDTH - 1
    base = V7X_SUBLANES

    @pl.when(i == 0)
    def _():
        xbuf_ref[base - hist:base, :] = conv0_ref[0]
        h_ref[...] = h0_ref[0]

    xbuf_ref[base:base + tt, :] = xr_ref[0]
    xc = cb_ref[...]
    for j in range(CONV_WIDTH):
        xc = xc + cw_ref[j:j + 1, :] * xbuf_ref[base - hist + j:base - hist + j + tt, :]
    tail = xbuf_ref[base + tt - hist:base + tt, :]
    conv_out_ref[0] = tail
    xbuf_ref[base - hist:base, :] = tail

    r = jax.nn.sigmoid(_dot(xc, wa_ref[...]) + ba_ref[...])
    gate_i = jax.nn.sigmoid(_dot(xc, wx_ref[...]) + bx_ref[...])
    log_a = -RG_C * r * _softplus(-lam_ref[...])
    a = jnp.exp(log_a)
    b_in = jnp.sqrt(-jnp.tanh(log_a) * (a * a + 1.0)) * (gate_i * xc)
    h = _scan_rows(a, b_in, h_ref[...])
    h_last = h[tt - 1:tt, :]
    h_ref[...] = h_last
    h_out_ref[0] = h_last
    rg_ref[0] = _rms(h * jax.nn.gelu(yb_ref[0]), g_ref[...])


def _block_diag(w):
    n, r, c = w.shape
    eye = jnp.eye(n, dtype=w.dtype)
    return (eye[:, None, :, None] * w[:, :, None, :]).reshape(n * r, n * c)


def _rglru(xr, yb, conv0, h0, p, *, tt):
    b, t, c = xr.shape
    hist = CONV_WIDTH - 1
    wa = _block_diag(p["rg_wa"]).astype(_BF16)
    wx = _block_diag(p["rg_wx"]).astype(_BF16)
    row = lambda v: v.reshape(1, c)
    tile = pl.BlockSpec((1, tt, c), lambda bi, i: (bi, i, 0))
    const = lambda shape: pl.BlockSpec(shape, lambda bi, i: (0,) * len(shape))
    return pl.pallas_call(
        functools.partial(_rglru_kernel, tt=tt),
        out_shape=[jax.ShapeDtypeStruct((b, t, c), _F32),
                   jax.ShapeDtypeStruct((b, hist, c), _F32),
                   jax.ShapeDtypeStruct((b, 1, c), _F32)],
        grid=(b, t // tt),
        in_specs=[tile, tile,
                  pl.BlockSpec((1, hist, c), lambda bi, i: (bi, 0, 0)),
                  pl.BlockSpec((1, 1, c), lambda bi, i: (bi, 0, 0)),
                  const((CONV_WIDTH, c)), const((1, c)), const((c, c)), const((1, c)),
                  const((c, c)), const((1, c)), const((1, c)), const((1, c))],
        out_specs=[tile,
                   pl.BlockSpec((1, hist, c), lambda bi, i: (bi, 0, 0)),
                   pl.BlockSpec((1, 1, c), lambda bi, i: (bi, 0, 0))],
        scratch_shapes=[pltpu.VMEM((V7X_SUBLANES + tt, c), _F32), pltpu.VMEM((1, c), _F32)],
        compiler_params=_params("parallel", "arbitrary"),
        name="rglru",
    )(xr, yb, conv0, h0.reshape(b, 1, c), p["conv_w"], row(p["conv_b"]), wa, row(p["rg_ba"]),
      wx, row(p["rg_bx"]), row(p["rg_lambda"]), row(p["norm_rg"]))


def _route(logits):
    lane = lax.broadcasted_iota(jnp.int32, logits.shape, 1)
    lane_f = lane.astype(_F32)
    far = float(V7X_LANES)
    neg = -jnp.inf
    is_group = (lane >= N_EXPERTS) & (lane < N_EXPERTS + N_GROUPS)
    g_logit = jnp.where(is_group, logits, neg)
    g_max = jnp.max(g_logit, axis=-1, keepdims=True)
    grp = jnp.min(jnp.where(g_logit == g_max, lane_f - N_EXPERTS, far), axis=-1, keepdims=True)
    p_sel = 1.0 / jnp.sum(jnp.where(is_group, jnp.exp(logits - g_max), 0.0), axis=-1, keepdims=True)
    in_group = _div_pow2(lane, EXPERTS_PER_GROUP).astype(_F32) == grp
    e_logit = jnp.where(in_group, logits, neg)
    v1 = jnp.max(e_logit, axis=-1, keepdims=True)
    i1 = jnp.min(jnp.where(e_logit == v1, lane_f, far), axis=-1, keepdims=True)
    e_rest = jnp.where(lane_f == i1, neg, e_logit)
    v2 = jnp.max(e_rest, axis=-1, keepdims=True)
    i2 = jnp.min(jnp.where(e_rest == v2, lane_f, far), axis=-1, keepdims=True)
    t = jnp.exp(v2 - v1)
    g1 = p_sel / (1.0 + t)
    g2 = p_sel * t / (1.0 + t)
    return jnp.where(lane == 0, i1, jnp.where(lane == 1, i2, jnp.where(lane == 2, g1, g2)))


def _outproj_kernel(x_ref, attn_ref, rg_ref, gate_ref, scale_ref, shift_ref, wo_a_ref, wo_r_ref,
                    g_ref, wr_ref, br_ref, x2_ref, xn2_ref, route_ref):
    mix = _dot(attn_ref[0], wo_a_ref[...]) + _dot(rg_ref[0], wo_r_ref[...])
    x2 = x_ref[0] + gate_ref[0] * mix
    xn2 = _rms(x2, g_ref[...]) * (1.0 + scale_ref[0]) + shift_ref[0]
    x2_ref[0] = x2
    xn2_ref[0] = xn2
    route_ref[0] = _route(_dot(xn2, wr_ref[...]) + br_ref[...])


def _outproj(x, attn, rg, gate, scale, shift, p, *, tm):
    g, t, d = x.shape
    w_attn = attn.shape[-1]
    wdt = _BF16
    wo_a = p["w_out"][:w_attn].astype(wdt)
    wo_r = p["w_out"][w_attn:].astype(wdt)
    pad = V7X_LANES - N_EXPERTS - N_GROUPS
    w_route = jnp.pad(jnp.concatenate([p["w_router"], p["w_group"]], axis=1), ((0, 0), (0, pad))).astype(wdt)
    b_route = jnp.pad(jnp.concatenate([p["b_router"], p["b_group"]]), (0, pad)).reshape(1, V7X_LANES)
    mod_rows = gate.shape[1]
    mod_block = (1, 1, d) if mod_rows == 1 else (1, tm, d)
    mod_map = (lambda b, i: (b, 0, 0)) if mod_rows == 1 else (lambda b, i: (b, i, 0))
    mod_spec = pl.BlockSpec(mod_block, mod_map)
    tok = lambda w: pl.BlockSpec((1, tm, w), lambda b, i: (b, i, 0))
    const = lambda shape: pl.BlockSpec(shape, lambda b, i: (0,) * len(shape))
    return pl.pallas_call(
        _outproj_kernel,
        out_shape=[jax.ShapeDtypeStruct((g, t, d), _F32), jax.ShapeDtypeStruct((g, t, d), _F32),
                   jax.ShapeDtypeStruct((g, t, V7X_LANES), _F32)],
        grid=(g, t // tm),
        in_specs=[tok(d), tok(w_attn), tok(rg.shape[-1]), mod_spec, mod_spec, mod_spec,
                  const(wo_a.shape), const(wo_r.shape), const((1, d)), const(w_route.shape),
                  const((1, V7X_LANES))],
        out_specs=[tok(d), tok(d), tok(V7X_LANES)],
        compiler_params=_params("parallel", "parallel"),
        name="outproj",
    )(x, attn, rg, gate, scale, shift, wo_a, wo_r, p["norm_ffn"].reshape(1, d), w_route, b_route)


def _moe_ffn_kernel(be_ref, nvalid_ref, tok_ref, dst_ref, x_hbm, gate_ref, w1_ref, w3_ref, w2_ref,
                    y_hbm, xbuf_ref, ybuf_ref, sem_ref):
    del be_ref
    i = pl.program_id(0)
    n_valid = nvalid_ref[i]
    block = xbuf_ref.shape[0]

    def in_copy(r):
        return pltpu.make_async_copy(x_hbm.at[pl.ds(tok_ref[0, 0, r], 1)], xbuf_ref.at[pl.ds(r, 1)],
                                     sem_ref.at[0])

    def out_copy(r):
        return pltpu.make_async_copy(ybuf_ref.at[pl.ds(r, 1)], y_hbm.at[pl.ds(dst_ref[0, 0, r], 1)],
                                     sem_ref.at[1])

    def each(n, fn):
        lax.fori_loop(0, n, lambda r, _: (fn(r), 0)[1], 0)

    @pl.when(n_valid > 0)
    def _():
        each(block, lambda r: in_copy(r).start())
        each(block, lambda r: in_copy(r).wait())
        xb = xbuf_ref[...].astype(_BF16)
        up = _dot(xb, w1_ref[0])
        hidden = up * jax.nn.sigmoid(up) * _dot(xb, w3_ref[0])
        ybuf_ref[...] = _dot(hidden, w2_ref[0]) * gate_ref[0]
        each(n_valid, lambda r: out_copy(r).start())
        each(n_valid, lambda r: out_copy(r).wait())


def _moe_plan(experts, gates, block):
    n_slot = experts.size
    e = experts.reshape(n_slot)
    onehot = (e[:, None] == jnp.arange(N_EXPERTS, dtype=jnp.int32)[None, :]).astype(jnp.int32)
    before = jnp.cumsum(onehot, axis=0) - onehot
    rank = jnp.sum(before * onehot, axis=1)
    counts = jnp.sum(onehot, axis=0)
    padded = (counts + block - 1) // block * block
    pad_end = jnp.cumsum(padded)
    pad_start = pad_end - padded
    dest = pad_start[e] + rank
    n_blocks = (n_slot + N_EXPERTS * (block - 1) + block - 1) // block
    n_rows = n_blocks * block
    slot = jnp.arange(n_slot, dtype=jnp.int32)
    row_tok = jnp.zeros((n_rows,), jnp.int32).at[dest].set(slot // TOP_K)
    row_dst = jnp.zeros((n_rows,), jnp.int32).at[dest].set(slot)
    row_gate = jnp.zeros((n_rows,), _F32).at[dest].set(gates.reshape(n_slot))
    block_start = jnp.arange(n_blocks, dtype=jnp.int32) * block
    block_expert = jnp.minimum(jnp.searchsorted(pad_end, block_start, side="right"), N_EXPERTS - 1).astype(jnp.int32)
    n_valid = jnp.clip(pad_start[block_expert] + counts[block_expert] - block_start, 0, block).astype(jnp.int32)
    return (row_tok.reshape(n_blocks, 1, block), row_dst.reshape(n_blocks, 1, block),
            row_gate.reshape(n_blocks, block, 1), block_expert, n_valid)


def _moe_ffn(xn2, experts, gates, w1, w3, w2):
    n_tok, d = xn2.shape
    d_exp = w1.shape[-1]
    block = MOE_BLOCK
    row_tok, row_dst, row_gate, block_expert, n_valid = _moe_plan(experts, gates, block)
    n_blocks = row_tok.shape[0]
    idx_spec = pl.BlockSpec((1, 1, block), lambda i, be, nv: (i, 0, 0), memory_space=pltpu.SMEM)
    grid_spec = pltpu.PrefetchScalarGridSpec(
        num_scalar_prefetch=2,
        grid=(n_blocks,),
        in_specs=[idx_spec, idx_spec,
                  pl.BlockSpec(memory_space=pl.ANY),
                  pl.BlockSpec((1, block, 1), lambda i, be, nv: (i, 0, 0)),
                  pl.BlockSpec((1, d, d_exp), lambda i, be, nv: (be[i], 0, 0)),
                  pl.BlockSpec((1, d, d_exp), lambda i, be, nv: (be[i], 0, 0)),
                  pl.BlockSpec((1, d_exp, d), lambda i, be, nv: (be[i], 0, 0))],
        out_specs=pl.BlockSpec(memory_space=pl.ANY),
        scratch_shapes=[pltpu.VMEM((block, d), _F32), pltpu.VMEM((block, d), _F32),
                        pltpu.SemaphoreType.DMA((2,))],
    )

    y = pl.pallas_call(
        _moe_ffn_kernel,
        out_shape=jax.ShapeDtypeStruct((n_tok * TOP_K, d), _F32),
        grid_spec=grid_spec,
        compiler_params=_params("arbitrary"),
        name="moe_ffn",
    )(block_expert, n_valid, row_tok, row_dst, xn2, row_gate, w1, w3, w2)
    return y.reshape(n_tok, TOP_K * d)


def _final_kernel(x2_ref, y_ref, gate_ref, g_ref, o_ref):
    d = x2_ref.shape[-1]
    ffn = y_ref[:, :d]
    for k in range(1, TOP_K):
        ffn = ffn + y_ref[:, k * d:(k + 1) * d]
    o_ref[0] = _rms(x2_ref[0] + gate_ref[0] * ffn, g_ref[...])


def _final(x2, y_all, row_offset, gate, final_norm, *, tm):
    g, t, d = x2.shape
    mod_rows = gate.shape[1]
    mod_block = (1, 1, d) if mod_rows == 1 else (1, tm, d)
    mod_map = (lambda b, i: (b, 0, 0)) if mod_rows == 1 else (lambda b, i: (b, i, 0))
    steps = t // tm
    off = row_offset // tm
    tok = pl.BlockSpec((1, tm, d), lambda b, i: (b, i, 0))
    return pl.pallas_call(
        _final_kernel,
        out_shape=jax.ShapeDtypeStruct((g, t, d), _F32),
        grid=(g, steps),
        in_specs=[tok,
                  pl.BlockSpec((tm, TOP_K * d), lambda b, i: (off + b * steps + i, 0)),
                  pl.BlockSpec(mod_block, mod_map),
                  pl.BlockSpec((1, d), lambda b, i: (0, 0))],
        out_specs=tok,
        compiler_params=_params("parallel", "parallel"),
        name="final",
    )(x2, y_all, gate, final_norm.reshape(1, d))


def _token_tile(t, want):
    return want if t % want == 0 else t


def kernel(x_prompt, x_sample, cache_k, cache_v, state_conv, state_h, page_table, c_prompt, c_sample,
           w_ada, b_ada, norm_mix, norm_ffn, w_in, sb_bias, norm_attn, conv_w, conv_b, rg_wa, rg_ba,
           rg_wx, rg_bx, rg_lambda, norm_rg, w_out, w_group, b_group, w_router, b_router, w1, w3, w2,
           final_norm):
    depth = w_ada.shape[0]
    assert depth == 1, "the final RMSNorm is fused into the layer's second residual add"
    bp, tp, d = x_prompt.shape
    bs, ts, _ = x_sample.shape
    n_s = bs * ts
    xp = x_prompt
    xs = x_sample.reshape(1, n_s, d)
    outs = [[] for _ in range(8)]
    for l in range(depth):
        p = dict(w_ada=w_ada[l], b_ada=b_ada[l], norm_mix=norm_mix[l], norm_ffn=norm_ffn[l], w_in=w_in[l],
                 sb_bias=sb_bias[l], norm_attn=norm_attn[l], conv_w=conv_w[l], conv_b=conv_b[l],
                 rg_wa=rg_wa[l], rg_ba=rg_ba[l], rg_wx=rg_wx[l], rg_bx=rg_bx[l], rg_lambda=rg_lambda[l],
                 norm_rg=norm_rg[l], w_out=w_out[l], w_group=w_group[l], b_group=b_group[l],
                 w_router=w_router[l], b_router=b_router[l])
        w_attn = p["w_in"].shape[1] // 5
        n_heads = w_attn // HEAD_DIM
        c_rg = w_attn
        mod = _ada(jnp.concatenate([c_prompt, c_sample], axis=0), p["w_ada"], p["b_ada"])
        mod_p = mod[:bp].reshape(bp, 6, 1, d)
        mod_s = jnp.repeat(mod[bp:].reshape(bs, 6, d), ts, axis=0).reshape(n_s, 6, d)
        shift1_p, scale1_p, gate1_p, shift2_p, scale2_p, gate2_p = (mod_p[:, j] for j in range(6))
        shift1_s, scale1_s, gate1_s, shift2_s, scale2_s, gate2_s = (mod_s[None, :, j] for j in range(6))

        w_in16 = p["w_in"].astype(_BF16)

        tm_p = _token_tile(tp, 512)
        q16, k_p, v_p, xr_p, yb_p, k16, v16 = _inproj(xp, scale1_p, shift1_p, p["norm_mix"], w_in16, tm=tm_p)
        attn_p = _attn_prompt(q16, k16, v16, p["sb_bias"], p["norm_attn"])
        conv0 = jnp.zeros((bp, CONV_WIDTH - 1, c_rg), _F32)
        h0 = jnp.zeros((bp, c_rg), _F32)
        rg_p, conv_p, h_p = _rglru(xr_p, yb_p, conv0, h0, p, tt=_token_tile(tp, 512))
        x2_p, xn2_p, route_p = _outproj(xp, attn_p, rg_p, gate1_p, scale2_p, shift2_p, p, tm=tm_p)

        q_s, k_s, v_s, xr_s, yb_s, _, _ = _inproj(xs, scale1_s, shift1_s, p["norm_mix"], w_in16, tm=n_s)
        seq = lambda a: a.reshape(bs, ts, a.shape[-1])
        attn_s = _attn_sample(seq(q_s), seq(k_s), seq(v_s), cache_k[l], cache_v[l], page_table,
                              p["sb_bias"], p["norm_attn"])
        rg_s, conv_s, h_s = _rglru(seq(xr_s), seq(yb_s), state_conv[l], state_h[l], p, tt=ts)
        x2_s, xn2_s, route_s = _outproj(xs, attn_s.reshape(1, n_s, w_attn), rg_s.reshape(1, n_s, c_rg),
                                        gate1_s, scale2_s, shift2_s, p, tm=n_s)

        n_p = bp * tp
        xn2_all = jnp.concatenate([xn2_p.reshape(n_p, d), xn2_s.reshape(n_s, d)], axis=0)
        route = jnp.concatenate([route_p.reshape(n_p, V7X_LANES), route_s.reshape(n_s, V7X_LANES)], axis=0)
        experts = route[:, 0:TOP_K].astype(jnp.int32)
        gates = route[:, TOP_K:2 * TOP_K]
        y_all = _moe_ffn(xn2_all, experts, gates, w1[l], w3[l], w2[l])
        xp = _final(x2_p, y_all, 0, gate2_p, final_norm, tm=tm_p)
        xs = _final(x2_s, y_all, n_p, gate2_s, final_norm, tm=n_s)
        hd = (n_heads, HEAD_DIM)
        for lst, val in zip(outs, (k_p.reshape(bp, tp, *hd), v_p.reshape(bp, tp, *hd), conv_p,
                                   h_p.reshape(bp, c_rg), seq(k_s).reshape(bs, ts, *hd),
                                   seq(v_s).reshape(bs, ts, *hd), conv_s, h_s.reshape(bs, c_rg))):
            lst.append(val)
    stacked = [jnp.stack(o) for o in outs]
    return (xp, xs.reshape(bs, ts, d), *stacked)
```

```python
import functools

import jax
import jax.numpy as jnp
from jax import lax
from jax.experimental import pallas as pl
from jax.experimental.pallas import tpu as pltpu

HEAD_DIM = 64
N_RG_BLOCKS = 8
CONV_WIDTH = 4
RG_C = 8.0
N_GROUPS = 4
EXPERTS_PER_GROUP = 8
N_EXPERTS = N_GROUPS * EXPERTS_PER_GROUP
TOP_K = 2
RMS_EPS = 1e-6
PAGE_SIZE = 128

V7X_LANES = 128
V7X_SUBLANES = 8
VMEM_LIMIT = 48 * 1024 * 1024

MOE_BLOCK = 256
ATTN_TILE = 256
PAGES_PER_STEP = 16
MASKED_LOGIT = -1e30

_F32 = jnp.float32
_BF16 = jnp.bfloat16
_HIGHEST = lax.Precision.HIGHEST


def _dot(a, b):
    return jnp.dot(a.astype(_BF16), b.astype(_BF16), preferred_element_type=_F32)


def _dot_nt(a, b):
    return lax.dot_general(a.astype(_BF16), b.astype(_BF16), (((1,), (1,)), ((), ())),
                           preferred_element_type=_F32)


def _dot_f32(a, b):
    return jnp.dot(a, b, precision=_HIGHEST, preferred_element_type=_F32)


def _rms(x, gain):
    return x * lax.rsqrt(jnp.mean(x * x, axis=-1, keepdims=True) + RMS_EPS) * gain


def _softplus(z):
    return jnp.maximum(z, 0.0) + jnp.log1p(jnp.exp(-jnp.abs(z)))


def _softplus_logits(z):
    return jnp.maximum(z, 0.0) + jnp.log(1.0 + jnp.exp(-jnp.abs(z)))


def _div_pow2(x, n):
    assert n & (n - 1) == 0
    return lax.shift_right_logical(x, n.bit_length() - 1)


def _mod_pow2(x, n):
    assert n & (n - 1) == 0
    return lax.bitwise_and(x, n - 1)


def _params(*sem):
    return pltpu.CompilerParams(dimension_semantics=sem, vmem_limit_bytes=VMEM_LIMIT)


def _ada_kernel(c_ref, w_ref, b_ref, o_ref):
    c = c_ref[...]
    o_ref[...] = _dot(c * jax.nn.sigmoid(c), w_ref[...]) + b_ref[...]


def _ada(c, w_ada, b_ada):
    rows, d = c.shape
    n_chunks = w_ada.shape[1] // d
    return pl.pallas_call(
        _ada_kernel,
        out_shape=jax.ShapeDtypeStruct((rows, n_chunks * d), _F32),
        grid=(n_chunks,),
        in_specs=[pl.BlockSpec((rows, d), lambda j: (0, 0)),
                  pl.BlockSpec((d, d), lambda j: (0, j)),
                  pl.BlockSpec((1, d), lambda j: (0, j))],
        out_specs=pl.BlockSpec((rows, d), lambda j: (0, j)),
        compiler_params=_params("parallel"),
        name="ada",
    )(c, w_ada, b_ada.reshape(1, -1))


def _inproj_kernel(x_ref, scale_ref, shift_ref, g_ref, w_ref, q16_ref, k_ref, v_ref, xr_ref, yb_ref,
                   k16_ref, v16_ref, *, w_attn):
    xn = (_rms(x_ref[0], g_ref[...]) * (1.0 + scale_ref[0]) + shift_ref[0]).astype(_BF16)

    def col(j):
        return _dot(xn, w_ref[:, j * w_attn:(j + 1) * w_attn])

    q16_ref[0] = (col(0) * (HEAD_DIM ** -0.5)).astype(_BF16)
    k = col(1)
    v = col(2)
    k_ref[0] = k
    v_ref[0] = v
    k16_ref[0] = k.astype(_BF16)
    v16_ref[0] = v.astype(_BF16)
    xr_ref[0] = col(3)
    yb_ref[0] = col(4)


def _inproj(x, scale, shift, gain, w_in, *, tm):
    g, t, d = x.shape
    w_attn = w_in.shape[1] // 5
    mod_rows = scale.shape[1]
    mod_block = (1, 1, d) if mod_rows == 1 else (1, tm, d)
    mod_map = (lambda b, i: (b, 0, 0)) if mod_rows == 1 else (lambda b, i: (b, i, 0))
    out_block = pl.BlockSpec((1, tm, w_attn), lambda b, i: (b, i, 0))
    out_shape = [jax.ShapeDtypeStruct((g, t, w_attn), dt) for dt in (_BF16, _F32, _F32, _F32, _F32, _BF16, _BF16)]
    return pl.pallas_call(
        functools.partial(_inproj_kernel, w_attn=w_attn),
        out_shape=out_shape,
        grid=(g, t // tm),
        in_specs=[pl.BlockSpec((1, tm, d), lambda b, i: (b, i, 0)),
                  pl.BlockSpec(mod_block, mod_map),
                  pl.BlockSpec(mod_block, mod_map),
                  pl.BlockSpec((1, d), lambda b, i: (0, 0)),
                  pl.BlockSpec(w_in.shape, lambda b, i: (0, 0))],
        out_specs=[out_block] * len(out_shape),
        compiler_params=_params("parallel", "parallel"),
        name="inproj",
    )(x, scale, shift, gain.reshape(1, d), w_in)


def _head_pair_norm(acc, gain, head0):
    sq = acc * acc
    s0 = jnp.sum(jnp.where(head0, sq, 0.0), axis=-1, keepdims=True)
    s1 = jnp.sum(jnp.where(head0, 0.0, sq), axis=-1, keepdims=True)
    ms = jnp.where(head0, s0, s1) * (1.0 / HEAD_DIM)
    return acc * lax.rsqrt(ms + RMS_EPS) * gain


def _attn_prompt_kernel(bias_ref, q_ref, k_ref, v_ref, g_ref, o_ref, off_ref, z_ref, lb_ref, tail_ref,
                        rs_ref, acc_ref, c_ref, *, tile):
    hp = pl.program_id(1)
    qi = pl.program_id(2)
    lane = lax.broadcasted_iota(jnp.int32, (1, 2 * HEAD_DIM), 1)
    head0 = lane < HEAD_DIM
    q = q_ref[0]
    zero = jnp.zeros_like(q)
    q2 = jnp.concatenate([jnp.where(head0, q, zero), jnp.where(head0, zero, q)], axis=0)
    first = lax.broadcasted_iota(jnp.int32, (2 * tile, 1), 0) < tile
    bias = jnp.where(first, bias_ref[2 * hp], bias_ref[2 * hp + 1])
    row = lax.broadcasted_iota(jnp.int32, (tile, tile), 0)
    col = lax.broadcasted_iota(jnp.int32, (tile, tile), 1)
    suffix = (row > col).astype(_BF16)
    row2 = lax.broadcasted_iota(jnp.int32, (2 * tile, tile), 0)
    col2 = lax.broadcasted_iota(jnp.int32, (2 * tile, tile), 1)
    causal = col2 < _mod_pow2(row2, tile)
    off_ref[0] = jnp.broadcast_to(bias, (2 * tile, tile))
    off_ref[1] = jnp.where(causal, bias, MASKED_LOGIT)
    acc_ref[...] = jnp.zeros_like(acc_ref)
    c_ref[...] = jnp.zeros_like(c_ref)

    def rows_of(ref, kt):
        return ref[0, pl.ds(pl.multiple_of(kt * tile, tile), tile), :]

    def scores(kt):
        kt = jnp.maximum(kt, 0)
        z_ref[...] = _dot_nt(q2, rows_of(k_ref, kt)) + off_ref[(kt == qi).astype(jnp.int32)]

    def gates():
        z = z_ref[...]
        log1m = -_softplus_logits(z)
        lb_ref[...] = z + log1m
        tail_ref[...] = _dot(log1m, suffix)
        rs_ref[...] = jnp.sum(log1m, axis=-1, keepdims=True)

    def absorb(kt):
        w = jnp.exp(lb_ref[...] + tail_ref[...] + c_ref[...])
        acc_ref[...] += _dot(w, rows_of(v_ref, kt))
        c_ref[...] += rs_ref[...]

    scores(qi)
    gates()
    scores(qi - 1)

    def step(j, _):
        kt = qi - j
        absorb(kt)
        gates()
        scores(kt - 2)
        return 0

    lax.fori_loop(0, qi + 1, step, 0)
    acc = acc_ref[...]
    o_ref[0] = _head_pair_norm(jnp.where(head0, acc[:tile], acc[tile:]), g_ref[0], head0)


def _attn_prompt(q16, k16, v16, sb_bias, norm_attn):
    b, t, w_attn = q16.shape
    pair = 2 * HEAD_DIM
    n_pairs = w_attn // pair
    tile = min(ATTN_TILE, t)
    return pl.pallas_call(
        functools.partial(_attn_prompt_kernel, tile=tile),
        out_shape=jax.ShapeDtypeStruct((b, t, w_attn), _F32),
        grid=(b, n_pairs, t // tile),
        in_specs=[pl.BlockSpec(memory_space=pltpu.SMEM),
                  pl.BlockSpec((1, tile, pair), lambda bi, hp, qi: (bi, qi, hp)),
                  pl.BlockSpec((1, t, pair), lambda bi, hp, qi: (bi, 0, hp)),
                  pl.BlockSpec((1, t, pair), lambda bi, hp, qi: (bi, 0, hp)),
                  pl.BlockSpec((1, 1, pair), lambda bi, hp, qi: (hp, 0, 0))],
        out_specs=pl.BlockSpec((1, tile, pair), lambda bi, hp, qi: (bi, qi, hp)),
        scratch_shapes=[pltpu.VMEM((2, 2 * tile, tile), _F32),
                        pltpu.VMEM((2 * tile, tile), _F32),
                        pltpu.VMEM((2 * tile, tile), _F32),
                        pltpu.VMEM((2 * tile, tile), _F32),
                        pltpu.VMEM((2 * tile, 1), _F32),
                        pltpu.VMEM((2 * tile, 2 * HEAD_DIM), _F32),
                        pltpu.VMEM((2 * tile, 1), _F32)],
        compiler_params=_params("parallel", "parallel", "parallel"),
        name="attn_prompt",
    )(sb_bias, q16, k16, v16, norm_attn.reshape(n_pairs, 1, pair))


def _split3(x):
    hi = x.astype(_BF16)
    rest = x - hi.astype(_F32)
    mid = rest.astype(_BF16)
    return hi, mid, (rest - mid.astype(_F32)).astype(_BF16)


def _attn_sample_kernel(pt_ref, q_ref, knt_ref, vn_ref, bias_ref, g_ref, *rest, n_heads, t_new, pages):
    kt_refs = rest[:pages]
    vt_refs = rest[pages:2 * pages]
    o_ref, qbd_ref, acc_ref, c_ref = rest[2 * pages:]
    step = pl.program_id(1)
    rows = n_heads * t_new
    width = n_heads * HEAD_DIM
    r_id = lax.broadcasted_iota(jnp.int32, (rows, width), 0)
    l_id = lax.broadcasted_iota(jnp.int32, (rows, width), 1)
    own_head = _div_pow2(l_id, HEAD_DIM) == _div_pow2(r_id, t_new)
    bias = bias_ref[...]

    def suffix_matrix(n, dtype):
        return (lax.broadcasted_iota(jnp.int32, (n, n), 0) > lax.broadcasted_iota(jnp.int32, (n, n), 1)).astype(dtype)

    @pl.when(step == 0)
    def _():
        rep_r = lax.broadcasted_iota(jnp.int32, (rows, t_new), 0)
        rep_c = lax.broadcasted_iota(jnp.int32, (rows, t_new), 1)
        replicate = (_mod_pow2(rep_r, t_new) == rep_c).astype(_F32)
        q_rep = _dot(replicate, q_ref[0])
        qbd = jnp.where(own_head, q_rep, 0.0).astype(_BF16)
        qbd_ref[...] = qbd
        n_pad = knt_ref.shape[2]
        kr = _mod_pow2(lax.broadcasted_iota(jnp.int32, (rows, n_pad), 0), t_new)
        kc = lax.broadcasted_iota(jnp.int32, (rows, n_pad), 1)
        visible = kc < kr
        z = _dot(qbd, knt_ref[0]) + bias
        log1m = jnp.where(visible, -_softplus_logits(z), 0.0)
        tail = _dot_f32(log1m, suffix_matrix(n_pad, _F32))
        w = jnp.where(visible, jnp.exp(z + log1m + tail), 0.0)
        acc_ref[...] = _dot(w, vn_ref[0])
        c_ref[...] = jnp.sum(log1m, axis=-1, keepdims=True)

    qbd = qbd_ref[...]
    suffix = suffix_matrix(PAGE_SIZE, _BF16)
    zs = [_dot(qbd, kt_refs[j][0]) + bias for j in range(pages)]
    log1ms = [-_softplus_logits(z) for z in zs]
    tails = [sum(_dot(piece, suffix) for piece in _split3(l)) for l in log1ms]
    sums = [jnp.sum(l, axis=-1, keepdims=True) for l in log1ms]
    c = c_ref[...]
    acc = acc_ref[...]
    for j in range(pages):
        w = jnp.exp(zs[j] + log1ms[j] + tails[j] + c)
        acc = acc + _dot_nt(w, vt_refs[j][0])
        c = c + sums[j]
    acc_ref[...] = acc
    c_ref[...] = c

    @pl.when(step == pl.num_programs(1) - 1)
    def _():
        am = jnp.where(own_head, acc_ref[...], 0.0)
        ms = jnp.sum(am * am, axis=-1, keepdims=True) * (1.0 / HEAD_DIM)
        an = am * lax.rsqrt(ms + RMS_EPS)
        col_r = lax.broadcasted_iota(jnp.int32, (t_new, rows), 0)
        col_c = lax.broadcasted_iota(jnp.int32, (t_new, rows), 1)
        collect = (_mod_pow2(col_c, t_new) == col_r).astype(_F32)
        o_ref[0] = _dot_f32(collect, an) * g_ref[...]


def _attn_sample(q, k_new, v_new, cache_k, cache_v, page_table, sb_bias, norm_attn):
    s, t_new, width = q.shape
    n_heads = width // HEAD_DIM
    n_pages = page_table.shape[1]
    pages = min(PAGES_PER_STEP, n_pages)
    steps = n_pages // pages
    n_phys = cache_k.shape[0]
    ckt = cache_k.transpose(0, 2, 3, 1).reshape(n_phys, width, PAGE_SIZE)
    cvt = cache_v.transpose(0, 2, 3, 1).reshape(n_phys, width, PAGE_SIZE)
    n_pad = V7X_SUBLANES
    pad = ((0, 0), (0, n_pad - t_new), (0, 0))
    knt = jnp.pad(k_new, pad).transpose(0, 2, 1)
    vn = jnp.pad(v_new, pad)
    bias_rows = jnp.repeat(sb_bias, t_new).reshape(n_heads * t_new, 1)
    rows = n_heads * t_new

    def page_spec(j):
        return pl.BlockSpec((1, width, PAGE_SIZE),
                            lambda b, st, pt: (pt[b, n_pages - 1 - (st * pages + j)], 0, 0))

    seq_spec = lambda n: pl.BlockSpec((1, n, width), lambda b, st, pt: (b, 0, 0))
    grid_spec = pltpu.PrefetchScalarGridSpec(
        num_scalar_prefetch=1,
        grid=(s, steps),
        in_specs=[seq_spec(t_new),
                  pl.BlockSpec((1, width, n_pad), lambda b, st, pt: (b, 0, 0)),
                  seq_spec(n_pad),
                  pl.BlockSpec((rows, 1), lambda b, st, pt: (0, 0)),
                  pl.BlockSpec((1, width), lambda b, st, pt: (0, 0))]
                 + [page_spec(j) for j in range(pages)] * 2,
        out_specs=seq_spec(t_new),
        scratch_shapes=[pltpu.VMEM((rows, width), _BF16), pltpu.VMEM((rows, width), _F32),
                        pltpu.VMEM((rows, 1), _F32)],
    )
    return pl.pallas_call(
        functools.partial(_attn_sample_kernel, n_heads=n_heads, t_new=t_new, pages=pages),
        out_shape=jax.ShapeDtypeStruct((s, t_new, width), _F32),
        grid_spec=grid_spec,
        compiler_params=_params("parallel", "arbitrary"),
        name="attn_sample",
    )(page_table, q, knt, vn, bias_rows, norm_attn.reshape(1, width),
      *([ckt] * pages), *([cvt] * pages))


def _scan_rows(a, b, h_prev):
    n = a.shape[0]
    row = lax.broadcasted_iota(jnp.int32, (n, 1), 0)
    if n <= V7X_SUBLANES:
        h = jnp.zeros_like(b)
        for t in range(n):
            h_prev = a[t:t + 1] * h_prev + b[t:t + 1]
            h = jnp.where(row == t, h_prev, h)
        return h
    d = 1
    while d < n:
        keep = row >= d
        a_prev = jnp.where(keep, pltpu.roll(a, d, 0), 1.0)
        b_prev = jnp.where(keep, pltpu.roll(b, d, 0), 0.0)
        b = b + a * b_prev
        a = a * a_prev
        d *= 2
    return b + a * h_prev


def _rglru_kernel(xr_ref, yb_ref, conv0_ref, h0_ref, cw_ref, cb_ref, wa_ref, ba_ref, wx_ref, bx_ref,
                  lam_ref, g_ref, rg_ref, conv_out_ref, h_out_ref, xbuf_ref, h_ref, *, tt):
    i = pl.program_id(1)
    hist = CONV_WIDTH - 1
    base = V7X_SUBLANES

    @pl.when(i == 0)
    def _():
        xbuf_ref[base - hist:base, :] = conv0_ref[0]
        h_ref[...] = h0_ref[0]

    xbuf_ref[base:base + tt, :] = xr_ref[0]
    xc = cb_ref[...]
    for j in range(CONV_WIDTH):
        xc = xc + cw_ref[j:j + 1, :] * xbuf_ref[base - hist + j:base - hist + j + tt, :]
    tail = xbuf_ref[base + tt - hist:base + tt, :]
    conv_out_ref[0] = tail
    xbuf_ref[base - hist:base, :] = tail

    r = jax.nn.sigmoid(_dot(xc, wa_ref[...]) + ba_ref[...])
    gate_i = jax.nn.sigmoid(_dot(xc, wx_ref[...]) + bx_ref[...])
    log_a = -RG_C * r * _softplus(-lam_ref[...])
    a = jnp.exp(log_a)
    b_in = jnp.sqrt(-jnp.tanh(log_a) * (a * a + 1.0)) * (gate_i * xc)
    h = _scan_rows(a, b_in, h_ref[...])
    h_last = h[tt - 1:tt, :]
    h_ref[...] = h_last
    h_out_ref[0] = h_last
    rg_ref[0] = _rms(h * jax.nn.gelu(yb_ref[0]), g_ref[...])


def _block_diag(w):
    n, r, c = w.shape
    eye = jnp.eye(n, dtype=w.dtype)
    return (eye[:, None, :, None] * w[:, :, None, :]).reshape(n * r, n * c)


def _rglru(xr, yb, conv0, h0, p, *, tt):
    b, t, c = xr.shape
    hist = CONV_WIDTH - 1
    wa = _block_diag(p["rg_wa"]).astype(_BF16)
    wx = _block_diag(p["rg_wx"]).astype(_BF16)
    row = lambda v: v.reshape(1, c)
    tile = pl.BlockSpec((1, tt, c), lambda bi, i: (bi, i, 0))
    const = lambda shape: pl.BlockSpec(shape, lambda bi, i: (0,) * len(shape))
    return pl.pallas_call(
        functools.partial(_rglru_kernel, tt=tt),
        out_shape=[jax.ShapeDtypeStruct((b, t, c), _F32),
                   jax.ShapeDtypeStruct((b, hist, c), _F32),
                   jax.ShapeDtypeStruct((b, 1, c), _F32)],
        grid=(b, t // tt),
        in_specs=[tile, tile,
                  pl.BlockSpec((1, hist, c), lambda bi, i: (bi, 0, 0)),
                  pl.BlockSpec((1, 1, c), lambda bi, i: (bi, 0, 0)),
                  const((CONV_WIDTH, c)), const((1, c)), const((c, c)), const((1, c)),
                  const((c, c)), const((1, c)), const((1, c)), const((1, c))],
        out_specs=[tile,
                   pl.BlockSpec((1, hist, c), lambda bi, i: (bi, 0, 0)),
                   pl.BlockSpec((1, 1, c), lambda bi, i: (bi, 0, 0))],
        scratch_shapes=[pltpu.VMEM((V7X_SUBLANES + tt, c), _F32), pltpu.VMEM((1, c), _F32)],
        compiler_params=_params("parallel", "arbitrary"),
        name="rglru",
    )(xr, yb, conv0, h0.reshape(b, 1, c), p["conv_w"], row(p["conv_b"]), wa, row(p["rg_ba"]),
      wx, row(p["rg_bx"]), row(p["rg_lambda"]), row(p["norm_rg"]))


ROUTE_EXPERT, ROUTE_GATE, ROUTE_RANK = 0, TOP_K, 2 * TOP_K


def _route(logits, counts):
    lane = lax.broadcasted_iota(jnp.int32, logits.shape, 1)
    lane_f = lane.astype(_F32)
    far = float(V7X_LANES)
    neg = -jnp.inf
    is_group = (lane >= N_EXPERTS) & (lane < N_EXPERTS + N_GROUPS)
    g_logit = jnp.where(is_group, logits, neg)
    g_max = jnp.max(g_logit, axis=-1, keepdims=True)
    grp = jnp.min(jnp.where(g_logit == g_max, lane_f - N_EXPERTS, far), axis=-1, keepdims=True)
    p_sel = 1.0 / jnp.sum(jnp.where(is_group, jnp.exp(logits - g_max), 0.0), axis=-1, keepdims=True)
    in_group = _div_pow2(lane, EXPERTS_PER_GROUP).astype(_F32) == grp
    e_logit = jnp.where(in_group, logits, neg)
    v1 = jnp.max(e_logit, axis=-1, keepdims=True)
    i1 = jnp.min(jnp.where(e_logit == v1, lane_f, far), axis=-1, keepdims=True)
    e_rest = jnp.where(lane_f == i1, neg, e_logit)
    v2 = jnp.max(e_rest, axis=-1, keepdims=True)
    i2 = jnp.min(jnp.where(e_rest == v2, lane_f, far), axis=-1, keepdims=True)
    t = jnp.exp(v2 - v1)
    g1 = p_sel / (1.0 + t)
    g2 = p_sel * t / (1.0 + t)
    n = logits.shape[0]
    chosen = jnp.where(lane_f == i1, 1.0, jnp.where(lane_f == i2, 1.0, 0.0))
    earlier = (lax.broadcasted_iota(jnp.int32, (n, n), 0) > lax.broadcasted_iota(jnp.int32, (n, n), 1))
    before = _dot(earlier.astype(_BF16), chosen) + counts
    r1 = jnp.sum(jnp.where(lane_f == i1, before, 0.0), axis=-1, keepdims=True)
    r2 = jnp.sum(jnp.where(lane_f == i2, before, 0.0), axis=-1, keepdims=True)
    route = jnp.zeros_like(logits)
    for offset, values in ((ROUTE_EXPERT, (i1, i2)), (ROUTE_GATE, (g1, g2)), (ROUTE_RANK, (r1, r2))):
        for k, val in enumerate(values):
            route = jnp.where(lane == offset + k, val, route)
    return route, counts + jnp.sum(chosen, axis=0, keepdims=True)


def _outproj_kernel(x_ref, attn_ref, rg_ref, gate_ref, scale_ref, shift_ref, wo_a_ref, wo_r_ref,
                    g_ref, wr_ref, br_ref, x2_ref, xn2_ref, route_ref, counts_ref):
    @pl.when((pl.program_id(0) == 0) & (pl.program_id(1) == 0))
    def _():
        counts_ref[...] = jnp.zeros_like(counts_ref)

    mix = _dot(attn_ref[0], wo_a_ref[...]) + _dot(rg_ref[0], wo_r_ref[...])
    x2 = x_ref[0] + gate_ref[0] * mix
    xn2 = _rms(x2, g_ref[...]) * (1.0 + scale_ref[0]) + shift_ref[0]
    x2_ref[0] = x2
    xn2_ref[0] = xn2
    route_ref[0], counts_ref[...] = _route(_dot(xn2, wr_ref[...]) + br_ref[...], counts_ref[...])


def _outproj(x, attn, rg, gate, scale, shift, p, *, tm):
    g, t, d = x.shape
    w_attn = attn.shape[-1]
    wdt = _BF16
    wo_a = p["w_out"][:w_attn].astype(wdt)
    wo_r = p["w_out"][w_attn:].astype(wdt)
    pad = V7X_LANES - N_EXPERTS - N_GROUPS
    w_route = jnp.pad(jnp.concatenate([p["w_router"], p["w_group"]], axis=1), ((0, 0), (0, pad))).astype(wdt)
    b_route = jnp.pad(jnp.concatenate([p["b_router"], p["b_group"]]), (0, pad)).reshape(1, V7X_LANES)
    mod_rows = gate.shape[1]
    mod_block = (1, 1, d) if mod_rows == 1 else (1, tm, d)
    mod_map = (lambda b, i: (b, 0, 0)) if mod_rows == 1 else (lambda b, i: (b, i, 0))
    mod_spec = pl.BlockSpec(mod_block, mod_map)
    tok = lambda w: pl.BlockSpec((1, tm, w), lambda b, i: (b, i, 0))
    const = lambda shape: pl.BlockSpec(shape, lambda b, i: (0,) * len(shape))
    return pl.pallas_call(
        _outproj_kernel,
        out_shape=[jax.ShapeDtypeStruct((g, t, d), _F32), jax.ShapeDtypeStruct((g, t, d), _F32),
                   jax.ShapeDtypeStruct((g, t, V7X_LANES), _F32), jax.ShapeDtypeStruct((1, V7X_LANES), _F32)],
        grid=(g, t // tm),
        in_specs=[tok(d), tok(w_attn), tok(rg.shape[-1]), mod_spec, mod_spec, mod_spec,
                  const(wo_a.shape), const(wo_r.shape), const((1, d)), const(w_route.shape),
                  const((1, V7X_LANES))],
        out_specs=[tok(d), tok(d), tok(V7X_LANES), const((1, V7X_LANES))],
        compiler_params=_params("arbitrary", "arbitrary"),
        name="outproj",
    )(x, attn, rg, gate, scale, shift, wo_a, wo_r, p["norm_ffn"].reshape(1, d), w_route, b_route)


def _moe_ffn_kernel(be_ref, nvalid_ref, x_ref, w1_ref, w3_ref, w2_ref, y_ref, w1b_ref, w3b_ref, w2b_ref):
    i = pl.program_id(0)
    new_expert = (i == 0) | (be_ref[i] != be_ref[jnp.maximum(i - 1, 0)])

    @pl.when(nvalid_ref[i] > 0)
    def _():
        @pl.when(new_expert)
        def _():
            w1b_ref[...] = w1_ref[0].astype(_BF16)
            w3b_ref[...] = w3_ref[0].astype(_BF16)
            w2b_ref[...] = w2_ref[0].astype(_BF16)

        xb = x_ref[...].astype(_BF16)
        up = _dot(xb, w1b_ref[...])
        hidden = up * jax.nn.sigmoid(up) * _dot(xb, w3b_ref[...])
        y_ref[...] = _dot(hidden, w2b_ref[...])

    @pl.when(nvalid_ref[i] == 0)
    def _():
        y_ref[...] = jnp.zeros_like(y_ref)


def _moe_layout(counts, n_blocks, block):
    padded = (counts + block - 1) // block * block
    pad_end = jnp.cumsum(padded)
    pad_start = pad_end - padded
    block_start = jnp.arange(n_blocks, dtype=jnp.int32) * block
    block_expert = jnp.minimum(jnp.searchsorted(pad_end, block_start, side="right"), N_EXPERTS - 1).astype(jnp.int32)
    n_valid = jnp.clip(pad_start[block_expert] + counts[block_expert] - block_start, 0, block).astype(jnp.int32)
    return pad_start, block_expert, n_valid


def _slot_rows(route, first_row):
    experts = route[:, ROUTE_EXPERT:ROUTE_EXPERT + TOP_K].astype(jnp.int32)
    rank = route[:, ROUTE_RANK:ROUTE_RANK + TOP_K].astype(jnp.int32)
    return first_row[experts] + rank


def _dispatch_kernel(dest_ref, x_ref, rows_in_hbm, rows_hbm, sem_ref):
    del rows_in_hbm
    tm = x_ref.shape[1]

    def copy(j):
        return pltpu.make_async_copy(x_ref.at[0, pl.ds(_div_pow2(j, TOP_K), 1)],
                                     rows_hbm.at[pl.ds(dest_ref[0, 0, j], 1)], sem_ref.at[0])

    lax.fori_loop(0, tm * TOP_K, lambda j, _: (copy(j).start(), 0)[1], 0)
    lax.fori_loop(0, tm * TOP_K, lambda j, _: (copy(j).wait(), 0)[1], 0)


def _dispatch(xn2, dest, rows, *, tm):
    g, t, d = xn2.shape
    steps = t // tm
    dest_blocks = dest.reshape(g * steps, 1, tm * TOP_K)
    return pl.pallas_call(
        _dispatch_kernel,
        out_shape=jax.ShapeDtypeStruct(rows.shape, rows.dtype),
        grid=(g, steps),
        in_specs=[pl.BlockSpec((1, 1, tm * TOP_K), lambda b, i: (b * steps + i, 0, 0), memory_space=pltpu.SMEM),
                  pl.BlockSpec((1, tm, d), lambda b, i: (b, i, 0)),
                  pl.BlockSpec(memory_space=pl.ANY)],
        out_specs=pl.BlockSpec(memory_space=pl.ANY),
        scratch_shapes=[pltpu.SemaphoreType.DMA((1,))],
        input_output_aliases={2: 0},
        compiler_params=_params("arbitrary", "arbitrary"),
        name="moe_dispatch",
    )(dest_blocks, xn2, rows)


def _moe_ffn(rows, block_expert, n_valid, w1, w3, w2):
    n_rows, d = rows.shape
    d_exp = w1.shape[-1]
    block = MOE_BLOCK
    row_spec = pl.BlockSpec((block, d), lambda i, be, nv: (i, 0))
    grid_spec = pltpu.PrefetchScalarGridSpec(
        num_scalar_prefetch=2,
        grid=(n_rows // block,),
        in_specs=[row_spec,
                  pl.BlockSpec((1, d, d_exp), lambda i, be, nv: (be[i], 0, 0)),
                  pl.BlockSpec((1, d, d_exp), lambda i, be, nv: (be[i], 0, 0)),
                  pl.BlockSpec((1, d_exp, d), lambda i, be, nv: (be[i], 0, 0))],
        out_specs=row_spec,
        scratch_shapes=[pltpu.VMEM((d, d_exp), _BF16), pltpu.VMEM((d, d_exp), _BF16),
                        pltpu.VMEM((d_exp, d), _BF16)],
    )
    return pl.pallas_call(
        _moe_ffn_kernel,
        out_shape=jax.ShapeDtypeStruct((n_rows, d), _F32),
        grid_spec=grid_spec,
        compiler_params=_params("arbitrary"),
        name="moe_ffn",
    )(block_expert, n_valid, rows, w1, w3, w2)


def _final_kernel(dest_ref, x2_ref, route_ref, gate_ref, g_ref, y_hbm, o_ref, ybuf_ref, sem_ref):
    tm = x2_ref.shape[1]

    def copy(j):
        return pltpu.make_async_copy(y_hbm.at[pl.ds(dest_ref[0, 0, j], 1)],
                                     ybuf_ref.at[_mod_pow2(j, TOP_K), pl.ds(_div_pow2(j, TOP_K), 1)], sem_ref.at[0])

    lax.fori_loop(0, tm * TOP_K, lambda j, _: (copy(j).start(), 0)[1], 0)
    lax.fori_loop(0, tm * TOP_K, lambda j, _: (copy(j).wait(), 0)[1], 0)
    route = route_ref[0]
    ffn = route[:, ROUTE_GATE:ROUTE_GATE + 1] * ybuf_ref[0]
    for k in range(1, TOP_K):
        ffn = ffn + route[:, ROUTE_GATE + k:ROUTE_GATE + k + 1] * ybuf_ref[k]
    o_ref[0] = _rms(x2_ref[0] + gate_ref[0] * ffn, g_ref[...])


def _final(x2, route, dest, y_rows, gate, final_norm, *, tm):
    g, t, d = x2.shape
    mod_rows = gate.shape[1]
    mod_block = (1, 1, d) if mod_rows == 1 else (1, tm, d)
    mod_map = (lambda b, i: (b, 0, 0)) if mod_rows == 1 else (lambda b, i: (b, i, 0))
    steps = t // tm
    dest_blocks = dest.reshape(g * steps, 1, tm * TOP_K)
    tok = lambda w: pl.BlockSpec((1, tm, w), lambda b, i: (b, i, 0))
    return pl.pallas_call(
        _final_kernel,
        out_shape=jax.ShapeDtypeStruct((g, t, d), _F32),
        grid=(g, steps),
        in_specs=[pl.BlockSpec((1, 1, tm * TOP_K), lambda b, i: (b * steps + i, 0, 0), memory_space=pltpu.SMEM),
                  tok(d), tok(V7X_LANES),
                  pl.BlockSpec(mod_block, mod_map),
                  pl.BlockSpec((1, d), lambda b, i: (0, 0)),
                  pl.BlockSpec(memory_space=pl.ANY)],
        out_specs=tok(d),
        scratch_shapes=[pltpu.VMEM((TOP_K, tm, d), _F32), pltpu.SemaphoreType.DMA((1,))],
        compiler_params=_params("arbitrary", "arbitrary"),
        name="final",
    )(dest_blocks, x2, route, gate, final_norm.reshape(1, d), y_rows)


def _token_tile(t, want):
    return want if t % want == 0 else t


def kernel(x_prompt, x_sample, cache_k, cache_v, state_conv, state_h, page_table, c_prompt, c_sample,
           w_ada, b_ada, norm_mix, norm_ffn, w_in, sb_bias, norm_attn, conv_w, conv_b, rg_wa, rg_ba,
           rg_wx, rg_bx, rg_lambda, norm_rg, w_out, w_group, b_group, w_router, b_router, w1, w3, w2,
           final_norm):
    depth = w_ada.shape[0]
    assert depth == 1, "the final RMSNorm is fused into the layer's second residual add"
    bp, tp, d = x_prompt.shape
    bs, ts, _ = x_sample.shape
    n_s = bs * ts
    xp = x_prompt
    xs = x_sample.reshape(1, n_s, d)
    outs = [[] for _ in range(8)]
    for l in range(depth):
        p = dict(w_ada=w_ada[l], b_ada=b_ada[l], norm_mix=norm_mix[l], norm_ffn=norm_ffn[l], w_in=w_in[l],
                 sb_bias=sb_bias[l], norm_attn=norm_attn[l], conv_w=conv_w[l], conv_b=conv_b[l],
                 rg_wa=rg_wa[l], rg_ba=rg_ba[l], rg_wx=rg_wx[l], rg_bx=rg_bx[l], rg_lambda=rg_lambda[l],
                 norm_rg=norm_rg[l], w_out=w_out[l], w_group=w_group[l], b_group=b_group[l],
                 w_router=w_router[l], b_router=b_router[l])
        w_attn = p["w_in"].shape[1] // 5
        n_heads = w_attn // HEAD_DIM
        c_rg = w_attn
        mod = _ada(jnp.concatenate([c_prompt, c_sample], axis=0), p["w_ada"], p["b_ada"])
        mod_p = mod[:bp].reshape(bp, 6, 1, d)
        mod_s = jnp.repeat(mod[bp:].reshape(bs, 6, d), ts, axis=0).reshape(n_s, 6, d)
        shift1_p, scale1_p, gate1_p, shift2_p, scale2_p, gate2_p = (mod_p[:, j] for j in range(6))
        shift1_s, scale1_s, gate1_s, shift2_s, scale2_s, gate2_s = (mod_s[None, :, j] for j in range(6))

        w_in16 = p["w_in"].astype(_BF16)

        tm_p = _token_tile(tp, 512)
        q16, k_p, v_p, xr_p, yb_p, k16, v16 = _inproj(xp, scale1_p, shift1_p, p["norm_mix"], w_in16, tm=tm_p)
        attn_p = _attn_prompt(q16, k16, v16, p["sb_bias"], p["norm_attn"])
        conv0 = jnp.zeros((bp, CONV_WIDTH - 1, c_rg), _F32)
        h0 = jnp.zeros((bp, c_rg), _F32)
        rg_p, conv_p, h_p = _rglru(xr_p, yb_p, conv0, h0, p, tt=_token_tile(tp, 512))
        x2_p, xn2_p, route_p, counts_p = _outproj(xp, attn_p, rg_p, gate1_p, scale2_p, shift2_p, p, tm=tm_p)

        q_s, k_s, v_s, xr_s, yb_s, _, _ = _inproj(xs, scale1_s, shift1_s, p["norm_mix"], w_in16, tm=n_s)
        seq = lambda a: a.reshape(bs, ts, a.shape[-1])
        attn_s = _attn_sample(seq(q_s), seq(k_s), seq(v_s), cache_k[l], cache_v[l], page_table,
                              p["sb_bias"], p["norm_attn"])
        rg_s, conv_s, h_s = _rglru(seq(xr_s), seq(yb_s), state_conv[l], state_h[l], p, tt=ts)
        x2_s, xn2_s, route_s, counts_s = _outproj(xs, attn_s.reshape(1, n_s, w_attn), rg_s.reshape(1, n_s, c_rg),
                                                  gate1_s, scale2_s, shift2_s, p, tm=n_s)

        n_p = bp * tp
        n_blocks = pl.cdiv((n_p + n_s) * TOP_K + N_EXPERTS * (MOE_BLOCK - 1), MOE_BLOCK)
        slots_p = counts_p[0, :N_EXPERTS].astype(jnp.int32)
        slots_s = counts_s[0, :N_EXPERTS].astype(jnp.int32)
        first_row, block_expert, n_valid = _moe_layout(slots_p + slots_s, n_blocks, MOE_BLOCK)
        dest_p = _slot_rows(route_p.reshape(n_p, V7X_LANES), first_row)
        dest_s = _slot_rows(route_s.reshape(n_s, V7X_LANES), first_row + slots_p)
        rows = jnp.zeros((n_blocks * MOE_BLOCK, d), _F32)
        rows = _dispatch(xn2_p, dest_p, rows, tm=tm_p)
        rows = _dispatch(xn2_s, dest_s, rows, tm=n_s)
        y_rows = _moe_ffn(rows, block_expert, n_valid, w1[l], w3[l], w2[l])
        xp = _final(x2_p, route_p, dest_p, y_rows, gate2_p, final_norm, tm=tm_p)
        xs = _final(x2_s, route_s, dest_s, y_rows, gate2_s, final_norm, tm=n_s)
        hd = (n_heads, HEAD_DIM)
        for lst, val in zip(outs, (k_p.reshape(bp, tp, *hd), v_p.reshape(bp, tp, *hd), conv_p,
                                   h_p.reshape(bp, c_rg), seq(k_s).reshape(bs, ts, *hd),
                                   seq(v_s).reshape(bs, ts, *hd), conv_s, h_s.reshape(bs, c_rg))):
            lst.append(val)
    stacked = [jnp.stack(o) for o in outs]
    return (xp, xs.reshape(bs, ts, d), *stacked)
```

```python
import functools

import jax
import jax.numpy as jnp
from jax import lax
from jax.experimental import pallas as pl
from jax.experimental.pallas import tpu as pltpu

HEAD_DIM = 64
N_RG_BLOCKS = 8
CONV_WIDTH = 4
RG_C = 8.0
N_GROUPS = 4
EXPERTS_PER_GROUP = 8
N_EXPERTS = N_GROUPS * EXPERTS_PER_GROUP
TOP_K = 2
RMS_EPS = 1e-6
PAGE_SIZE = 128

V7X_LANES = 128
V7X_SUBLANES = 8
VMEM_LIMIT = 48 * 1024 * 1024
ATTN_VMEM_LIMIT = 56 * 1024 * 1024

MOE_BLOCK = 256
ATTN_TILE = 256
PAGES_PER_STEP = 16
MASKED_LOGIT = -1e30

_F32 = jnp.float32
_BF16 = jnp.bfloat16
_HIGHEST = lax.Precision.HIGHEST


def _dot(a, b):
    return jnp.dot(a.astype(_BF16), b.astype(_BF16), preferred_element_type=_F32)


def _dot_nt(a, b):
    return lax.dot_general(a.astype(_BF16), b.astype(_BF16), (((1,), (1,)), ((), ())),
                           preferred_element_type=_F32)


def _dot_f32(a, b):
    return jnp.dot(a, b, precision=_HIGHEST, preferred_element_type=_F32)


def _rms(x, gain):
    return x * lax.rsqrt(jnp.mean(x * x, axis=-1, keepdims=True) + RMS_EPS) * gain


def _softplus(z):
    return jnp.maximum(z, 0.0) + jnp.log1p(jnp.exp(-jnp.abs(z)))


def _softplus_logits(z):
    return jnp.maximum(z, 0.0) + jnp.log(1.0 + jnp.exp(-jnp.abs(z)))


def _div_pow2(x, n):
    assert n & (n - 1) == 0
    return lax.shift_right_logical(x, n.bit_length() - 1)


def _mod_pow2(x, n):
    assert n & (n - 1) == 0
    return lax.bitwise_and(x, n - 1)


def _params(*sem):
    return pltpu.CompilerParams(dimension_semantics=sem, vmem_limit_bytes=VMEM_LIMIT)


def _ada_kernel(c_ref, w_ref, b_ref, o_ref):
    c = c_ref[...]
    o_ref[...] = _dot(c * jax.nn.sigmoid(c), w_ref[...]) + b_ref[...]


def _ada(c, w_ada, b_ada):
    rows, d = c.shape
    n_chunks = w_ada.shape[1] // d
    return pl.pallas_call(
        _ada_kernel,
        out_shape=jax.ShapeDtypeStruct((rows, n_chunks * d), _F32),
        grid=(n_chunks,),
        in_specs=[pl.BlockSpec((rows, d), lambda j: (0, 0)),
                  pl.BlockSpec((d, d), lambda j: (0, j)),
                  pl.BlockSpec((1, d), lambda j: (0, j))],
        out_specs=pl.BlockSpec((rows, d), lambda j: (0, j)),
        compiler_params=_params("parallel"),
        name="ada",
    )(c, w_ada, b_ada.reshape(1, -1))


PAIR = 2 * HEAD_DIM


def _inproj_kernel(x_ref, scale_ref, shift_ref, g_ref, w_ref, k_ref, v_ref, xr_ref, yb_ref, q16_ref,
                   *kv16_refs, w_attn):
    xn = (_rms(x_ref[0], g_ref[...]) * (1.0 + scale_ref[0]) + shift_ref[0]).astype(_BF16)

    def col(j):
        return _dot(xn, w_ref[:, j * w_attn:(j + 1) * w_attn])

    q16 = (col(0) * (HEAD_DIM ** -0.5)).astype(_BF16)
    k = col(1)
    v = col(2)
    k_ref[0] = k
    v_ref[0] = v
    xr_ref[0] = col(3)
    yb_ref[0] = col(4)
    if kv16_refs:
        for ref, val in zip((q16_ref,) + kv16_refs, (q16, k.astype(_BF16), v.astype(_BF16))):
            for hp in range(w_attn // PAIR):
                ref[0, hp] = val[:, hp * PAIR:(hp + 1) * PAIR]
    else:
        q16_ref[0] = q16


def _inproj(x, scale, shift, gain, w_in, *, tm, pair_major):
    g, t, d = x.shape
    w_attn = w_in.shape[1] // 5
    mod_rows = scale.shape[1]
    mod_block = (1, 1, d) if mod_rows == 1 else (1, tm, d)
    mod_map = (lambda b, i: (b, 0, 0)) if mod_rows == 1 else (lambda b, i: (b, i, 0))
    out_block = pl.BlockSpec((1, tm, w_attn), lambda b, i: (b, i, 0))
    out_shape = [jax.ShapeDtypeStruct((g, t, w_attn), _F32)] * 4
    out_specs = [out_block] * 4
    if pair_major:
        n_pairs = w_attn // PAIR
        out_shape += [jax.ShapeDtypeStruct((g, n_pairs, t, PAIR), _BF16)] * 3
        out_specs += [pl.BlockSpec((1, n_pairs, tm, PAIR), lambda b, i: (b, 0, i, 0))] * 3
    else:
        out_shape += [jax.ShapeDtypeStruct((g, t, w_attn), _BF16)]
        out_specs += [out_block]
    return pl.pallas_call(
        functools.partial(_inproj_kernel, w_attn=w_attn),
        out_shape=out_shape,
        grid=(g, t // tm),
        in_specs=[pl.BlockSpec((1, tm, d), lambda b, i: (b, i, 0)),
                  pl.BlockSpec(mod_block, mod_map),
                  pl.BlockSpec(mod_block, mod_map),
                  pl.BlockSpec((1, d), lambda b, i: (0, 0)),
                  pl.BlockSpec(w_in.shape, lambda b, i: (0, 0))],
        out_specs=out_specs,
        compiler_params=_params("parallel", "parallel"),
        name="inproj",
    )(x, scale, shift, gain.reshape(1, d), w_in)


def _head_pair_norm(acc, gain, head0):
    sq = acc * acc
    s0 = jnp.sum(jnp.where(head0, sq, 0.0), axis=-1, keepdims=True)
    s1 = jnp.sum(jnp.where(head0, 0.0, sq), axis=-1, keepdims=True)
    ms = jnp.where(head0, s0, s1) * (1.0 / HEAD_DIM)
    return acc * lax.rsqrt(ms + RMS_EPS) * gain


def _attn_prompt_kernel(bias_ref, q_ref, k_ref, v_ref, g_ref, o_ref, off_ref, suffix_ref, q2_ref,
                        z_ref, lb_ref, tail_ref, rs_ref, w_ref, acc_ref, c_ref, *, tile, n_pairs):
    qi = pl.program_id(1)
    n_kt = qi + 1
    lane = lax.broadcasted_iota(jnp.int32, (1, PAIR), 1)
    head0 = lane < HEAD_DIM

    @pl.when((pl.program_id(0) == 0) & (qi == 0))
    def _():
        row = lax.broadcasted_iota(jnp.int32, (tile, tile), 0)
        col = lax.broadcasted_iota(jnp.int32, (tile, tile), 1)
        suffix_ref[...] = (row > col).astype(_BF16)
        row2 = lax.broadcasted_iota(jnp.int32, (2 * tile, tile), 0)
        col2 = lax.broadcasted_iota(jnp.int32, (2 * tile, tile), 1)
        causal = col2 < _mod_pow2(row2, tile)
        first = lax.broadcasted_iota(jnp.int32, (2 * tile, 1), 0) < tile
        for hp in range(n_pairs):
            bias = jnp.where(first, bias_ref[2 * hp], bias_ref[2 * hp + 1])
            off_ref[hp, 0] = jnp.broadcast_to(bias, (2 * tile, tile))
            off_ref[hp, 1] = jnp.where(causal, bias, MASKED_LOGIT)

    for hp in range(n_pairs):
        q = q_ref[0, hp]
        zero = jnp.zeros_like(q)
        q2_ref[hp] = jnp.concatenate([jnp.where(head0, q, zero), jnp.where(head0, zero, q)], axis=0)
    acc_ref[...] = jnp.zeros_like(acc_ref)
    c_ref[...] = jnp.zeros_like(c_ref)

    assert n_pairs % 2 == 0
    per_stream = n_pairs // 2

    def advance(hp, i):
        i = i + 1
        wrap = (i == n_kt).astype(jnp.int32)
        hp = hp + wrap
        i = i * (1 - wrap)
        done = (hp == per_stream).astype(jnp.int32)
        return hp - done, i + done * (n_kt - 1)

    def rows_of(ref, hp, i):
        return ref[0, hp, pl.ds(pl.multiple_of((qi - i) * tile, tile), tile), :]

    def scores(x, hp, i):
        hp = x * per_stream + hp
        z_ref[x] = _dot_nt(q2_ref[hp], rows_of(k_ref, hp, i)) + off_ref[hp, (i == 0).astype(jnp.int32)]

    def gates(x):
        z = z_ref[x]
        log1m = -_softplus_logits(z)
        lb_ref[x] = z + log1m
        log1m = log1m.astype(_BF16)
        tail = _dot(log1m, suffix_ref[...])
        tail_ref[x] = tail
        rs_ref[x] = jnp.broadcast_to(tail[:, :1] + log1m[:, :1].astype(_F32), rs_ref.shape[1:])

    def weights(x, hp):
        hp = x * per_stream + hp
        c = c_ref[hp]
        visited = jnp.concatenate([c] * (tile // PAIR), axis=1)
        w_ref[x] = jnp.exp(lb_ref[x] + tail_ref[x] + visited).astype(_BF16)
        c_ref[hp] = c + rs_ref[x]

    def absorb(x, hp, i):
        hp = x * per_stream + hp
        acc_ref[hp] += _dot(w_ref[x], rows_of(v_ref, hp, i))

    zero = jnp.int32(0)
    t0 = (zero, zero)
    t1 = advance(*t0)
    t2 = advance(*t1)
    for x in range(2):
        scores(x, *t0)
    for x in range(2):
        gates(x)
        scores(x, *t1)
    for x in range(2):
        weights(x, t0[0])
        gates(x)
        scores(x, *t2)

    def step(_, tiles):
        oldest, older, newest = tiles
        nxt = advance(*newest)
        for x in range(2):
            absorb(x, *oldest)
            weights(x, older[0])
            gates(x)
            scores(x, *nxt)
        return older, newest, nxt

    lax.fori_loop(0, per_stream * n_kt, step, (t0, t1, t2))
    for hp in range(n_pairs):
        acc = acc_ref[hp]
        o_ref[0, :, hp * PAIR:(hp + 1) * PAIR] = _head_pair_norm(
            jnp.where(head0, acc[:tile], acc[tile:]), g_ref[hp], head0)


def _attn_prompt(q16, k16, v16, sb_bias, norm_attn):
    b, n_pairs, t, _ = q16.shape
    tile = min(ATTN_TILE, t)
    stage = lambda dtype: pltpu.VMEM((2, 2 * tile, tile), dtype)
    return pl.pallas_call(
        functools.partial(_attn_prompt_kernel, tile=tile, n_pairs=n_pairs),
        out_shape=jax.ShapeDtypeStruct((b, t, n_pairs * PAIR), _F32),
        grid=(b, t // tile),
        in_specs=[pl.BlockSpec(memory_space=pltpu.SMEM),
                  pl.BlockSpec((1, n_pairs, tile, PAIR), lambda bi, qi: (bi, 0, qi, 0)),
                  pl.BlockSpec((1, n_pairs, t, PAIR), lambda bi, qi: (bi, 0, 0, 0)),
                  pl.BlockSpec((1, n_pairs, t, PAIR), lambda bi, qi: (bi, 0, 0, 0)),
                  pl.BlockSpec((n_pairs, 1, PAIR), lambda bi, qi: (0, 0, 0))],
        out_specs=pl.BlockSpec((1, tile, n_pairs * PAIR), lambda bi, qi: (bi, qi, 0)),
        scratch_shapes=[pltpu.VMEM((n_pairs, 2, 2 * tile, tile), _F32),
                        pltpu.VMEM((tile, tile), _BF16),
                        pltpu.VMEM((n_pairs, 2 * tile, PAIR), _BF16),
                        stage(_F32),
                        stage(_F32),
                        stage(_F32),
                        pltpu.VMEM((2, 2 * tile, PAIR), _F32),
                        stage(_BF16),
                        pltpu.VMEM((n_pairs, 2 * tile, PAIR), _F32),
                        pltpu.VMEM((n_pairs, 2 * tile, PAIR), _F32)],
        compiler_params=pltpu.CompilerParams(dimension_semantics=("arbitrary", "arbitrary"),
                                             vmem_limit_bytes=ATTN_VMEM_LIMIT),
        name="attn_prompt",
    )(sb_bias, q16, k16, v16, norm_attn.reshape(n_pairs, 1, PAIR))


def _split3(x):
    hi = x.astype(_BF16)
    rest = x - hi.astype(_F32)
    mid = rest.astype(_BF16)
    return hi, mid, (rest - mid.astype(_F32)).astype(_BF16)


def _attn_sample_kernel(pt_ref, q_ref, knt_ref, vn_ref, bias_ref, g_ref, *rest, n_heads, t_new, pages):
    kt_refs = rest[:pages]
    vt_refs = rest[pages:2 * pages]
    o_ref, qbd_ref, acc_ref, c_ref = rest[2 * pages:]
    step = pl.program_id(1)
    rows = n_heads * t_new
    width = n_heads * HEAD_DIM
    r_id = lax.broadcasted_iota(jnp.int32, (rows, width), 0)
    l_id = lax.broadcasted_iota(jnp.int32, (rows, width), 1)
    own_head = _div_pow2(l_id, HEAD_DIM) == _div_pow2(r_id, t_new)
    bias = bias_ref[...]

    def suffix_matrix(n, dtype):
        return (lax.broadcasted_iota(jnp.int32, (n, n), 0) > lax.broadcasted_iota(jnp.int32, (n, n), 1)).astype(dtype)

    @pl.when(step == 0)
    def _():
        rep_r = lax.broadcasted_iota(jnp.int32, (rows, t_new), 0)
        rep_c = lax.broadcasted_iota(jnp.int32, (rows, t_new), 1)
        replicate = (_mod_pow2(rep_r, t_new) == rep_c).astype(_F32)
        q_rep = _dot(replicate, q_ref[0])
        qbd = jnp.where(own_head, q_rep, 0.0).astype(_BF16)
        qbd_ref[...] = qbd
        n_pad = knt_ref.shape[2]
        kr = _mod_pow2(lax.broadcasted_iota(jnp.int32, (rows, n_pad), 0), t_new)
        kc = lax.broadcasted_iota(jnp.int32, (rows, n_pad), 1)
        visible = kc < kr
        z = _dot(qbd, knt_ref[0]) + bias
        log1m = jnp.where(visible, -_softplus_logits(z), 0.0)
        tail = _dot_f32(log1m, suffix_matrix(n_pad, _F32))
        w = jnp.where(visible, jnp.exp(z + log1m + tail), 0.0)
        acc_ref[...] = _dot(w, vn_ref[0])
        c_ref[...] = jnp.sum(log1m, axis=-1, keepdims=True)

    qbd = qbd_ref[...]
    suffix = suffix_matrix(PAGE_SIZE, _BF16)
    zs = [_dot(qbd, kt_refs[j][0]) + bias for j in range(pages)]
    log1ms = [-_softplus_logits(z) for z in zs]
    tails = [sum(_dot(piece, suffix) for piece in _split3(l)) for l in log1ms]
    sums = [jnp.sum(l, axis=-1, keepdims=True) for l in log1ms]
    c = c_ref[...]
    acc = acc_ref[...]
    for j in range(pages):
        w = jnp.exp(zs[j] + log1ms[j] + tails[j] + c)
        acc = acc + _dot_nt(w, vt_refs[j][0])
        c = c + sums[j]
    acc_ref[...] = acc
    c_ref[...] = c

    @pl.when(step == pl.num_programs(1) - 1)
    def _():
        am = jnp.where(own_head, acc_ref[...], 0.0)
        ms = jnp.sum(am * am, axis=-1, keepdims=True) * (1.0 / HEAD_DIM)
        an = am * lax.rsqrt(ms + RMS_EPS)
        col_r = lax.broadcasted_iota(jnp.int32, (t_new, rows), 0)
        col_c = lax.broadcasted_iota(jnp.int32, (t_new, rows), 1)
        collect = (_mod_pow2(col_c, t_new) == col_r).astype(_F32)
        o_ref[0] = _dot_f32(collect, an) * g_ref[...]


def _attn_sample(q, k_new, v_new, cache_k, cache_v, page_table, sb_bias, norm_attn):
    s, t_new, width = q.shape
    n_heads = width // HEAD_DIM
    n_pages = page_table.shape[1]
    pages = min(PAGES_PER_STEP, n_pages)
    steps = n_pages // pages
    n_phys = cache_k.shape[0]
    ckt = cache_k.transpose(0, 2, 3, 1).reshape(n_phys, width, PAGE_SIZE)
    cvt = cache_v.transpose(0, 2, 3, 1).reshape(n_phys, width, PAGE_SIZE)
    n_pad = V7X_SUBLANES
    pad = ((0, 0), (0, n_pad - t_new), (0, 0))
    knt = jnp.pad(k_new, pad).transpose(0, 2, 1)
    vn = jnp.pad(v_new, pad)
    bias_rows = jnp.repeat(sb_bias, t_new).reshape(n_heads * t_new, 1)
    rows = n_heads * t_new

    def page_spec(j):
        return pl.BlockSpec((1, width, PAGE_SIZE),
                            lambda b, st, pt: (pt[b, n_pages - 1 - (st * pages + j)], 0, 0))

    seq_spec = lambda n: pl.BlockSpec((1, n, width), lambda b, st, pt: (b, 0, 0))
    grid_spec = pltpu.PrefetchScalarGridSpec(
        num_scalar_prefetch=1,
        grid=(s, steps),
        in_specs=[seq_spec(t_new),
                  pl.BlockSpec((1, width, n_pad), lambda b, st, pt: (b, 0, 0)),
                  seq_spec(n_pad),
                  pl.BlockSpec((rows, 1), lambda b, st, pt: (0, 0)),
                  pl.BlockSpec((1, width), lambda b, st, pt: (0, 0))]
                 + [page_spec(j) for j in range(pages)] * 2,
        out_specs=seq_spec(t_new),
        scratch_shapes=[pltpu.VMEM((rows, width), _BF16), pltpu.VMEM((rows, width), _F32),
                        pltpu.VMEM((rows, 1), _F32)],
    )
    return pl.pallas_call(
        functools.partial(_attn_sample_kernel, n_heads=n_heads, t_new=t_new, pages=pages),
        out_shape=jax.ShapeDtypeStruct((s, t_new, width), _F32),
        grid_spec=grid_spec,
        compiler_params=_params("parallel", "arbitrary"),
        name="attn_sample",
    )(page_table, q, knt, vn, bias_rows, norm_attn.reshape(1, width),
      *([ckt] * pages), *([cvt] * pages))


def _scan_rows(a, b, h_prev):
    n = a.shape[0]
    row = lax.broadcasted_iota(jnp.int32, (n, 1), 0)
    if n <= V7X_SUBLANES:
        h = jnp.zeros_like(b)
        for t in range(n):
            h_prev = a[t:t + 1] * h_prev + b[t:t + 1]
            h = jnp.where(row == t, h_prev, h)
        return h
    d = 1
    while d < n:
        keep = row >= d
        a_prev = jnp.where(keep, pltpu.roll(a, d, 0), 1.0)
        b_prev = jnp.where(keep, pltpu.roll(b, d, 0), 0.0)
        b = b + a * b_prev
        a = a * a_prev
        d *= 2
    return b + a * h_prev


def _rglru_kernel(xr_ref, yb_ref, conv0_ref, h0_ref, cw_ref, cb_ref, wa_ref, ba_ref, wx_ref, bx_ref,
                  lam_ref, g_ref, rg_ref, conv_out_ref, h_out_ref, xbuf_ref, h_ref, *, tt):
    i = pl.program_id(1)
    hist = CONV_WIDTH - 1
    base = V7X_SUBLANES

    @pl.when(i == 0)
    def _():
        xbuf_ref[base - hist:base, :] = conv0_ref[0]
        h_ref[...] = h0_ref[0]

    xbuf_ref[base:base + tt, :] = xr_ref[0]
    xc = cb_ref[...]
    for j in range(CONV_WIDTH):
        xc = xc + cw_ref[j:j + 1, :] * xbuf_ref[base - hist + j:base - hist + j + tt, :]
    tail = xbuf_ref[base + tt - hist:base + tt, :]
    conv_out_ref[0] = tail
    xbuf_ref[base - hist:base, :] = tail

    r = jax.nn.sigmoid(_dot(xc, wa_ref[...]) + ba_ref[...])
    gate_i = jax.nn.sigmoid(_dot(xc, wx_ref[...]) + bx_ref[...])
    log_a = -RG_C * r * _softplus(-lam_ref[...])
    a = jnp.exp(log_a)
    b_in = jnp.sqrt(-jnp.tanh(log_a) * (a * a + 1.0)) * (gate_i * xc)
    h = _scan_rows(a, b_in, h_ref[...])
    h_last = h[tt - 1:tt, :]
    h_ref[...] = h_last
    h_out_ref[0] = h_last
    rg_ref[0] = _rms(h * jax.nn.gelu(yb_ref[0]), g_ref[...])


def _block_diag(w):
    n, r, c = w.shape
    eye = jnp.eye(n, dtype=w.dtype)
    return (eye[:, None, :, None] * w[:, :, None, :]).reshape(n * r, n * c)


def _rglru(xr, yb, conv0, h0, p, *, tt):
    b, t, c = xr.shape
    hist = CONV_WIDTH - 1
    wa = _block_diag(p["rg_wa"]).astype(_BF16)
    wx = _block_diag(p["rg_wx"]).astype(_BF16)
    row = lambda v: v.reshape(1, c)
    tile = pl.BlockSpec((1, tt, c), lambda bi, i: (bi, i, 0))
    const = lambda shape: pl.BlockSpec(shape, lambda bi, i: (0,) * len(shape))
    return pl.pallas_call(
        functools.partial(_rglru_kernel, tt=tt),
        out_shape=[jax.ShapeDtypeStruct((b, t, c), _F32),
                   jax.ShapeDtypeStruct((b, hist, c), _F32),
                   jax.ShapeDtypeStruct((b, 1, c), _F32)],
        grid=(b, t // tt),
        in_specs=[tile, tile,
                  pl.BlockSpec((1, hist, c), lambda bi, i: (bi, 0, 0)),
                  pl.BlockSpec((1, 1, c), lambda bi, i: (bi, 0, 0)),
                  const((CONV_WIDTH, c)), const((1, c)), const((c, c)), const((1, c)),
                  const((c, c)), const((1, c)), const((1, c)), const((1, c))],
        out_specs=[tile,
                   pl.BlockSpec((1, hist, c), lambda bi, i: (bi, 0, 0)),
                   pl.BlockSpec((1, 1, c), lambda bi, i: (bi, 0, 0))],
        scratch_shapes=[pltpu.VMEM((V7X_SUBLANES + tt, c), _F32), pltpu.VMEM((1, c), _F32)],
        compiler_params=_params("parallel", "arbitrary"),
        name="rglru",
    )(xr, yb, conv0, h0.reshape(b, 1, c), p["conv_w"], row(p["conv_b"]), wa, row(p["rg_ba"]),
      wx, row(p["rg_bx"]), row(p["rg_lambda"]), row(p["norm_rg"]))


ROUTE_EXPERT, ROUTE_GATE, ROUTE_RANK = 0, TOP_K, 2 * TOP_K


def _route(logits, counts):
    lane = lax.broadcasted_iota(jnp.int32, logits.shape, 1)
    lane_f = lane.astype(_F32)
    far = float(V7X_LANES)
    neg = -jnp.inf
    is_group = (lane >= N_EXPERTS) & (lane < N_EXPERTS + N_GROUPS)
    g_logit = jnp.where(is_group, logits, neg)
    g_max = jnp.max(g_logit, axis=-1, keepdims=True)
    grp = jnp.min(jnp.where(g_logit == g_max, lane_f - N_EXPERTS, far), axis=-1, keepdims=True)
    p_sel = 1.0 / jnp.sum(jnp.where(is_group, jnp.exp(logits - g_max), 0.0), axis=-1, keepdims=True)
    in_group = _div_pow2(lane, EXPERTS_PER_GROUP).astype(_F32) == grp
    e_logit = jnp.where(in_group, logits, neg)
    v1 = jnp.max(e_logit, axis=-1, keepdims=True)
    i1 = jnp.min(jnp.where(e_logit == v1, lane_f, far), axis=-1, keepdims=True)
    e_rest = jnp.where(lane_f == i1, neg, e_logit)
    v2 = jnp.max(e_rest, axis=-1, keepdims=True)
    i2 = jnp.min(jnp.where(e_rest == v2, lane_f, far), axis=-1, keepdims=True)
    t = jnp.exp(v2 - v1)
    g1 = p_sel / (1.0 + t)
    g2 = p_sel * t / (1.0 + t)
    n = logits.shape[0]
    chosen = jnp.where(lane_f == i1, 1.0, jnp.where(lane_f == i2, 1.0, 0.0))
    earlier = (lax.broadcasted_iota(jnp.int32, (n, n), 0) > lax.broadcasted_iota(jnp.int32, (n, n), 1))
    before = _dot(earlier.astype(_BF16), chosen) + counts
    r1 = jnp.sum(jnp.where(lane_f == i1, before, 0.0), axis=-1, keepdims=True)
    r2 = jnp.sum(jnp.where(lane_f == i2, before, 0.0), axis=-1, keepdims=True)
    route = jnp.zeros_like(logits)
    for offset, values in ((ROUTE_EXPERT, (i1, i2)), (ROUTE_GATE, (g1, g2)), (ROUTE_RANK, (r1, r2))):
        for k, val in enumerate(values):
            route = jnp.where(lane == offset + k, val, route)
    return route, counts + jnp.sum(chosen, axis=0, keepdims=True)


def _outproj_kernel(x_ref, attn_ref, rg_ref, gate_ref, scale_ref, shift_ref, wo_a_ref, wo_r_ref,
                    g_ref, wr_ref, br_ref, x2_ref, xn2_ref, route_ref, counts_ref):
    @pl.when((pl.program_id(0) == 0) & (pl.program_id(1) == 0))
    def _():
        counts_ref[...] = jnp.zeros_like(counts_ref)

    mix = _dot(attn_ref[0], wo_a_ref[...]) + _dot(rg_ref[0], wo_r_ref[...])
    x2 = x_ref[0] + gate_ref[0] * mix
    xn2 = _rms(x2, g_ref[...]) * (1.0 + scale_ref[0]) + shift_ref[0]
    x2_ref[0] = x2
    xn2_ref[0] = xn2
    route_ref[0], counts_ref[...] = _route(_dot(xn2, wr_ref[...]) + br_ref[...], counts_ref[...])


def _outproj(x, attn, rg, gate, scale, shift, p, *, tm):
    g, t, d = x.shape
    w_attn = attn.shape[-1]
    wdt = _BF16
    wo_a = p["w_out"][:w_attn].astype(wdt)
    wo_r = p["w_out"][w_attn:].astype(wdt)
    pad = V7X_LANES - N_EXPERTS - N_GROUPS
    w_route = jnp.pad(jnp.concatenate([p["w_router"], p["w_group"]], axis=1), ((0, 0), (0, pad))).astype(wdt)
    b_route = jnp.pad(jnp.concatenate([p["b_router"], p["b_group"]]), (0, pad)).reshape(1, V7X_LANES)
    mod_rows = gate.shape[1]
    mod_block = (1, 1, d) if mod_rows == 1 else (1, tm, d)
    mod_map = (lambda b, i: (b, 0, 0)) if mod_rows == 1 else (lambda b, i: (b, i, 0))
    mod_spec = pl.BlockSpec(mod_block, mod_map)
    tok = lambda w: pl.BlockSpec((1, tm, w), lambda b, i: (b, i, 0))
    const = lambda shape: pl.BlockSpec(shape, lambda b, i: (0,) * len(shape))
    return pl.pallas_call(
        _outproj_kernel,
        out_shape=[jax.ShapeDtypeStruct((g, t, d), _F32), jax.ShapeDtypeStruct((g, t, d), _F32),
                   jax.ShapeDtypeStruct((g, t, V7X_LANES), _F32), jax.ShapeDtypeStruct((1, V7X_LANES), _F32)],
        grid=(g, t // tm),
        in_specs=[tok(d), tok(w_attn), tok(rg.shape[-1]), mod_spec, mod_spec, mod_spec,
                  const(wo_a.shape), const(wo_r.shape), const((1, d)), const(w_route.shape),
                  const((1, V7X_LANES))],
        out_specs=[tok(d), tok(d), tok(V7X_LANES), const((1, V7X_LANES))],
        compiler_params=_params("arbitrary", "arbitrary"),
        name="outproj",
    )(x, attn, rg, gate, scale, shift, wo_a, wo_r, p["norm_ffn"].reshape(1, d), w_route, b_route)


def _moe_ffn_kernel(be_ref, nvalid_ref, x_ref, w1_ref, w3_ref, w2_ref, y_ref, w1b_ref, w3b_ref, w2b_ref):
    i = pl.program_id(0)
    new_expert = (i == 0) | (be_ref[i] != be_ref[jnp.maximum(i - 1, 0)])

    @pl.when(nvalid_ref[i] > 0)
    def _():
        @pl.when(new_expert)
        def _():
            w1b_ref[...] = w1_ref[0].astype(_BF16)
            w3b_ref[...] = w3_ref[0].astype(_BF16)
            w2b_ref[...] = w2_ref[0].astype(_BF16)

        xb = x_ref[...].astype(_BF16)
        up = _dot(xb, w1b_ref[...])
        hidden = up * jax.nn.sigmoid(up) * _dot(xb, w3b_ref[...])
        y_ref[...] = _dot(hidden, w2b_ref[...])

    @pl.when(nvalid_ref[i] == 0)
    def _():
        y_ref[...] = jnp.zeros_like(y_ref)


def _moe_layout(counts, n_blocks, block):
    padded = (counts + block - 1) // block * block
    pad_end = jnp.cumsum(padded)
    pad_start = pad_end - padded
    block_start = jnp.arange(n_blocks, dtype=jnp.int32) * block
    block_expert = jnp.minimum(jnp.searchsorted(pad_end, block_start, side="right"), N_EXPERTS - 1).astype(jnp.int32)
    n_valid = jnp.clip(pad_start[block_expert] + counts[block_expert] - block_start, 0, block).astype(jnp.int32)
    return pad_start, block_expert, n_valid


def _slot_rows(route, first_row):
    experts = route[:, ROUTE_EXPERT:ROUTE_EXPERT + TOP_K].astype(jnp.int32)
    rank = route[:, ROUTE_RANK:ROUTE_RANK + TOP_K].astype(jnp.int32)
    return first_row[experts] + rank


COPY_UNROLL = 16


def _for_each_copy(n, fn):
    assert n % COPY_UNROLL == 0
    lax.fori_loop(0, n, lambda j, _: (fn(j), 0)[1], 0, unroll=COPY_UNROLL)


def _dispatch_kernel(dest_ref, x_ref, rows_in_hbm, rows_hbm, sem_ref):
    del rows_in_hbm
    tm = x_ref.shape[1]

    def copy(j):
        return pltpu.make_async_copy(x_ref.at[0, pl.ds(_div_pow2(j, TOP_K), 1)],
                                     rows_hbm.at[pl.ds(dest_ref[0, 0, j], 1)], sem_ref.at[0])

    _for_each_copy(tm * TOP_K, lambda j: copy(j).start())
    _for_each_copy(tm * TOP_K, lambda j: copy(j).wait())


def _dispatch(xn2, dest, rows, *, tm):
    g, t, d = xn2.shape
    steps = t // tm
    dest_blocks = dest.reshape(g * steps, 1, tm * TOP_K)
    return pl.pallas_call(
        _dispatch_kernel,
        out_shape=jax.ShapeDtypeStruct(rows.shape, rows.dtype),
        grid=(g, steps),
        in_specs=[pl.BlockSpec((1, 1, tm * TOP_K), lambda b, i: (b * steps + i, 0, 0), memory_space=pltpu.SMEM),
                  pl.BlockSpec((1, tm, d), lambda b, i: (b, i, 0)),
                  pl.BlockSpec(memory_space=pl.ANY)],
        out_specs=pl.BlockSpec(memory_space=pl.ANY),
        scratch_shapes=[pltpu.SemaphoreType.DMA((1,))],
        input_output_aliases={2: 0},
        compiler_params=_params("arbitrary", "arbitrary"),
        name="moe_dispatch",
    )(dest_blocks, xn2, rows)


def _moe_ffn(rows, block_expert, n_valid, w1, w3, w2):
    n_rows, d = rows.shape
    d_exp = w1.shape[-1]
    block = MOE_BLOCK
    row_spec = pl.BlockSpec((block, d), lambda i, be, nv: (i, 0))
    grid_spec = pltpu.PrefetchScalarGridSpec(
        num_scalar_prefetch=2,
        grid=(n_rows // block,),
        in_specs=[row_spec,
                  pl.BlockSpec((1, d, d_exp), lambda i, be, nv: (be[i], 0, 0)),
                  pl.BlockSpec((1, d, d_exp), lambda i, be, nv: (be[i], 0, 0)),
                  pl.BlockSpec((1, d_exp, d), lambda i, be, nv: (be[i], 0, 0))],
        out_specs=row_spec,
        scratch_shapes=[pltpu.VMEM((d, d_exp), _BF16), pltpu.VMEM((d, d_exp), _BF16),
                        pltpu.VMEM((d_exp, d), _BF16)],
    )
    return pl.pallas_call(
        _moe_ffn_kernel,
        out_shape=jax.ShapeDtypeStruct((n_rows, d), _F32),
        grid_spec=grid_spec,
        compiler_params=_params("arbitrary"),
        name="moe_ffn",
    )(block_expert, n_valid, rows, w1, w3, w2)


def _final_kernel(dest_ref, x2_ref, route_ref, gate_ref, g_ref, y_hbm, o_ref, ybuf_ref, sem_ref):
    tm = x2_ref.shape[1]

    def copy(j):
        return pltpu.make_async_copy(y_hbm.at[pl.ds(dest_ref[0, 0, j], 1)],
                                     ybuf_ref.at[_mod_pow2(j, TOP_K), pl.ds(_div_pow2(j, TOP_K), 1)], sem_ref.at[0])

    _for_each_copy(tm * TOP_K, lambda j: copy(j).start())
    _for_each_copy(tm * TOP_K, lambda j: copy(j).wait())
    route = route_ref[0]
    ffn = route[:, ROUTE_GATE:ROUTE_GATE + 1] * ybuf_ref[0]
    for k in range(1, TOP_K):
        ffn = ffn + route[:, ROUTE_GATE + k:ROUTE_GATE + k + 1] * ybuf_ref[k]
    o_ref[0] = _rms(x2_ref[0] + gate_ref[0] * ffn, g_ref[...])


def _final(x2, route, dest, y_rows, gate, final_norm, *, tm):
    g, t, d = x2.shape
    mod_rows = gate.shape[1]
    mod_block = (1, 1, d) if mod_rows == 1 else (1, tm, d)
    mod_map = (lambda b, i: (b, 0, 0)) if mod_rows == 1 else (lambda b, i: (b, i, 0))
    steps = t // tm
    dest_blocks = dest.reshape(g * steps, 1, tm * TOP_K)
    tok = lambda w: pl.BlockSpec((1, tm, w), lambda b, i: (b, i, 0))
    return pl.pallas_call(
        _final_kernel,
        out_shape=jax.ShapeDtypeStruct((g, t, d), _F32),
        grid=(g, steps),
        in_specs=[pl.BlockSpec((1, 1, tm * TOP_K), lambda b, i: (b * steps + i, 0, 0), memory_space=pltpu.SMEM),
                  tok(d), tok(V7X_LANES),
                  pl.BlockSpec(mod_block, mod_map),
                  pl.BlockSpec((1, d), lambda b, i: (0, 0)),
                  pl.BlockSpec(memory_space=pl.ANY)],
        out_specs=tok(d),
        scratch_shapes=[pltpu.VMEM((TOP_K, tm, d), _F32), pltpu.SemaphoreType.DMA((1,))],
        compiler_params=_params("arbitrary", "arbitrary"),
        name="final",
    )(dest_blocks, x2, route, gate, final_norm.reshape(1, d), y_rows)


def _token_tile(t, want):
    return want if t % want == 0 else t


def kernel(x_prompt, x_sample, cache_k, cache_v, state_conv, state_h, page_table, c_prompt, c_sample,
           w_ada, b_ada, norm_mix, norm_ffn, w_in, sb_bias, norm_attn, conv_w, conv_b, rg_wa, rg_ba,
           rg_wx, rg_bx, rg_lambda, norm_rg, w_out, w_group, b_group, w_router, b_router, w1, w3, w2,
           final_norm):
    depth = w_ada.shape[0]
    assert depth == 1, "the final RMSNorm is fused into the layer's second residual add"
    bp, tp, d = x_prompt.shape
    bs, ts, _ = x_sample.shape
    n_s = bs * ts
    xp = x_prompt
    xs = x_sample.reshape(1, n_s, d)
    outs = [[] for _ in range(8)]
    for l in range(depth):
        p = dict(w_ada=w_ada[l], b_ada=b_ada[l], norm_mix=norm_mix[l], norm_ffn=norm_ffn[l], w_in=w_in[l],
                 sb_bias=sb_bias[l], norm_attn=norm_attn[l], conv_w=conv_w[l], conv_b=conv_b[l],
                 rg_wa=rg_wa[l], rg_ba=rg_ba[l], rg_wx=rg_wx[l], rg_bx=rg_bx[l], rg_lambda=rg_lambda[l],
                 norm_rg=norm_rg[l], w_out=w_out[l], w_group=w_group[l], b_group=b_group[l],
                 w_router=w_router[l], b_router=b_router[l])
        w_attn = p["w_in"].shape[1] // 5
        n_heads = w_attn // HEAD_DIM
        c_rg = w_attn
        mod = _ada(jnp.concatenate([c_prompt, c_sample], axis=0), p["w_ada"], p["b_ada"])
        mod_p = mod[:bp].reshape(bp, 6, 1, d)
        mod_s = jnp.repeat(mod[bp:].reshape(bs, 6, d), ts, axis=0).reshape(n_s, 6, d)
        shift1_p, scale1_p, gate1_p, shift2_p, scale2_p, gate2_p = (mod_p[:, j] for j in range(6))
        shift1_s, scale1_s, gate1_s, shift2_s, scale2_s, gate2_s = (mod_s[None, :, j] for j in range(6))

        w_in16 = p["w_in"].astype(_BF16)

        tm_p = _token_tile(tp, 512)
        k_p, v_p, xr_p, yb_p, q16, k16, v16 = _inproj(xp, scale1_p, shift1_p, p["norm_mix"], w_in16, tm=tm_p,
                                                      pair_major=True)
        attn_p = _attn_prompt(q16, k16, v16, p["sb_bias"], p["norm_attn"])
        conv0 = jnp.zeros((bp, CONV_WIDTH - 1, c_rg), _F32)
        h0 = jnp.zeros((bp, c_rg), _F32)
        rg_p, conv_p, h_p = _rglru(xr_p, yb_p, conv0, h0, p, tt=_token_tile(tp, 512))
        x2_p, xn2_p, route_p, counts_p = _outproj(xp, attn_p, rg_p, gate1_p, scale2_p, shift2_p, p, tm=tm_p)

        k_s, v_s, xr_s, yb_s, q_s = _inproj(xs, scale1_s, shift1_s, p["norm_mix"], w_in16, tm=n_s,
                                            pair_major=False)
        seq = lambda a: a.reshape(bs, ts, a.shape[-1])
        attn_s = _attn_sample(seq(q_s), seq(k_s), seq(v_s), cache_k[l], cache_v[l], page_table,
                              p["sb_bias"], p["norm_attn"])
        rg_s, conv_s, h_s = _rglru(seq(xr_s), seq(yb_s), state_conv[l], state_h[l], p, tt=ts)
        x2_s, xn2_s, route_s, counts_s = _outproj(xs, attn_s.reshape(1, n_s, w_attn), rg_s.reshape(1, n_s, c_rg),
                                                  gate1_s, scale2_s, shift2_s, p, tm=n_s)

        n_p = bp * tp
        n_blocks = pl.cdiv((n_p + n_s) * TOP_K + N_EXPERTS * (MOE_BLOCK - 1), MOE_BLOCK)
        slots_p = counts_p[0, :N_EXPERTS].astype(jnp.int32)
        slots_s = counts_s[0, :N_EXPERTS].astype(jnp.int32)
        first_row, block_expert, n_valid = _moe_layout(slots_p + slots_s, n_blocks, MOE_BLOCK)
        dest_p = _slot_rows(route_p.reshape(n_p, V7X_LANES), first_row)
        dest_s = _slot_rows(route_s.reshape(n_s, V7X_LANES), first_row + slots_p)
        rows = jnp.zeros((n_blocks * MOE_BLOCK, d), _F32)
        rows = _dispatch(xn2_p, dest_p, rows, tm=tm_p)
        rows = _dispatch(xn2_s, dest_s, rows, tm=n_s)
        y_rows = _moe_ffn(rows, block_expert, n_valid, w1[l], w3[l], w2[l])
        xp = _final(x2_p, route_p, dest_p, y_rows, gate2_p, final_norm, tm=tm_p)
        xs = _final(x2_s, route_s, dest_s, y_rows, gate2_s, final_norm, tm=n_s)
        hd = (n_heads, HEAD_DIM)
        for lst, val in zip(outs, (k_p.reshape(bp, tp, *hd), v_p.reshape(bp, tp, *hd), conv_p,
                                   h_p.reshape(bp, c_rg), seq(k_s).reshape(bs, ts, *hd),
                                   seq(v_s).reshape(bs, ts, *hd), conv_s, h_s.reshape(bs, c_rg))):
            lst.append(val)
    stacked = [jnp.stack(o) for o in outs]
    return (xp, xs.reshape(bs, ts, d), *stacked)
```

```python
import functools

import jax
import jax.numpy as jnp
from jax import lax
from jax.experimental import pallas as pl
from jax.experimental.pallas import tpu as pltpu

HEAD_DIM = 64
N_RG_BLOCKS = 8
CONV_WIDTH = 4
RG_C = 8.0
N_GROUPS = 4
EXPERTS_PER_GROUP = 8
N_EXPERTS = N_GROUPS * EXPERTS_PER_GROUP
TOP_K = 2
RMS_EPS = 1e-6
PAGE_SIZE = 128

V7X_LANES = 128
V7X_SUBLANES = 8
VMEM_LIMIT = 48 * 1024 * 1024
ATTN_VMEM_LIMIT = 56 * 1024 * 1024

MOE_BLOCK = 256
ATTN_TILE = 256
PAGES_PER_STEP = 16
MASKED_LOGIT = -1e30

_F32 = jnp.float32
_BF16 = jnp.bfloat16
_HIGHEST = lax.Precision.HIGHEST


def _dot(a, b):
    return jnp.dot(a.astype(_BF16), b.astype(_BF16), preferred_element_type=_F32)


def _dot_nt(a, b):
    return lax.dot_general(a.astype(_BF16), b.astype(_BF16), (((1,), (1,)), ((), ())),
                           preferred_element_type=_F32)


def _dot_f32(a, b):
    return jnp.dot(a, b, precision=_HIGHEST, preferred_element_type=_F32)


def _rms(x, gain):
    return x * lax.rsqrt(jnp.mean(x * x, axis=-1, keepdims=True) + RMS_EPS) * gain


def _softplus(z):
    return jnp.maximum(z, 0.0) + jnp.log1p(jnp.exp(-jnp.abs(z)))


def _softplus_logits(z):
    return jnp.maximum(z, 0.0) + jnp.log(1.0 + jnp.exp(-jnp.abs(z)))


def _div_pow2(x, n):
    assert n & (n - 1) == 0
    return lax.shift_right_logical(x, n.bit_length() - 1)


def _mod_pow2(x, n):
    assert n & (n - 1) == 0
    return lax.bitwise_and(x, n - 1)


def _params(*sem):
    return pltpu.CompilerParams(dimension_semantics=sem, vmem_limit_bytes=VMEM_LIMIT)


def _ada_kernel(c_ref, w_ref, b_ref, o_ref):
    c = c_ref[...]
    o_ref[...] = _dot(c * jax.nn.sigmoid(c), w_ref[...]) + b_ref[...]


def _ada(c, w_ada, b_ada):
    rows, d = c.shape
    n_chunks = w_ada.shape[1] // d
    return pl.pallas_call(
        _ada_kernel,
        out_shape=jax.ShapeDtypeStruct((rows, n_chunks * d), _F32),
        grid=(n_chunks,),
        in_specs=[pl.BlockSpec((rows, d), lambda j: (0, 0)),
                  pl.BlockSpec((d, d), lambda j: (0, j)),
                  pl.BlockSpec((1, d), lambda j: (0, j))],
        out_specs=pl.BlockSpec((rows, d), lambda j: (0, j)),
        compiler_params=_params("parallel"),
        name="ada",
    )(c, w_ada, b_ada.reshape(1, -1))


PAIR = 2 * HEAD_DIM


def _inproj_kernel(x_ref, scale_ref, shift_ref, g_ref, w_ref, k_ref, v_ref, xr_ref, yb_ref, q16_ref,
                   *kv16_refs, w_attn):
    xn = (_rms(x_ref[0], g_ref[...]) * (1.0 + scale_ref[0]) + shift_ref[0]).astype(_BF16)

    def col(j):
        return _dot(xn, w_ref[:, j * w_attn:(j + 1) * w_attn])

    q16 = (col(0) * (HEAD_DIM ** -0.5)).astype(_BF16)
    k = col(1)
    v = col(2)
    k_ref[0] = k
    v_ref[0] = v
    xr_ref[0] = col(3)
    yb_ref[0] = col(4)
    if kv16_refs:
        for ref, val in zip((q16_ref,) + kv16_refs, (q16, k.astype(_BF16), v.astype(_BF16))):
            for hp in range(w_attn // PAIR):
                ref[0, hp] = val[:, hp * PAIR:(hp + 1) * PAIR]
    else:
        q16_ref[0] = q16


def _inproj(x, scale, shift, gain, w_in, *, tm, pair_major):
    g, t, d = x.shape
    w_attn = w_in.shape[1] // 5
    mod_rows = scale.shape[1]
    mod_block = (1, 1, d) if mod_rows == 1 else (1, tm, d)
    mod_map = (lambda b, i: (b, 0, 0)) if mod_rows == 1 else (lambda b, i: (b, i, 0))
    out_block = pl.BlockSpec((1, tm, w_attn), lambda b, i: (b, i, 0))
    out_shape = [jax.ShapeDtypeStruct((g, t, w_attn), _F32)] * 4
    out_specs = [out_block] * 4
    if pair_major:
        n_pairs = w_attn // PAIR
        out_shape += [jax.ShapeDtypeStruct((g, n_pairs, t, PAIR), _BF16)] * 3
        out_specs += [pl.BlockSpec((1, n_pairs, tm, PAIR), lambda b, i: (b, 0, i, 0))] * 3
    else:
        out_shape += [jax.ShapeDtypeStruct((g, t, w_attn), _BF16)]
        out_specs += [out_block]
    return pl.pallas_call(
        functools.partial(_inproj_kernel, w_attn=w_attn),
        out_shape=out_shape,
        grid=(g, t // tm),
        in_specs=[pl.BlockSpec((1, tm, d), lambda b, i: (b, i, 0)),
                  pl.BlockSpec(mod_block, mod_map),
                  pl.BlockSpec(mod_block, mod_map),
                  pl.BlockSpec((1, d), lambda b, i: (0, 0)),
                  pl.BlockSpec(w_in.shape, lambda b, i: (0, 0))],
        out_specs=out_specs,
        compiler_params=_params("parallel", "parallel"),
        name="inproj",
    )(x, scale, shift, gain.reshape(1, d), w_in)


def _head_pair_norm(acc, gain, head0):
    sq = acc * acc
    s0 = jnp.sum(jnp.where(head0, sq, 0.0), axis=-1, keepdims=True)
    s1 = jnp.sum(jnp.where(head0, 0.0, sq), axis=-1, keepdims=True)
    ms = jnp.where(head0, s0, s1) * (1.0 / HEAD_DIM)
    return acc * lax.rsqrt(ms + RMS_EPS) * gain


def _attn_prompt_kernel(bias_ref, q_ref, k_ref, v_ref, g_ref, o_ref, off_ref, suffix_ref, q2_ref,
                        z_ref, lb_ref, tail_ref, rs_ref, w_ref, acc_ref, c_ref, *, tile, n_pairs):
    qi = pl.program_id(1)
    n_kt = qi + 1
    lane = lax.broadcasted_iota(jnp.int32, (1, PAIR), 1)
    head0 = lane < HEAD_DIM

    @pl.when((pl.program_id(0) == 0) & (qi == 0))
    def _():
        row = lax.broadcasted_iota(jnp.int32, (tile, tile), 0)
        col = lax.broadcasted_iota(jnp.int32, (tile, tile), 1)
        suffix_ref[...] = (row > col).astype(_BF16)
        row2 = lax.broadcasted_iota(jnp.int32, (2 * tile, tile), 0)
        col2 = lax.broadcasted_iota(jnp.int32, (2 * tile, tile), 1)
        causal = col2 < _mod_pow2(row2, tile)
        first = lax.broadcasted_iota(jnp.int32, (2 * tile, 1), 0) < tile
        for hp in range(n_pairs):
            bias = jnp.where(first, bias_ref[2 * hp], bias_ref[2 * hp + 1])
            off_ref[hp, 0] = jnp.broadcast_to(bias, (2 * tile, tile))
            off_ref[hp, 1] = jnp.where(causal, bias, MASKED_LOGIT)

    for hp in range(n_pairs):
        q = q_ref[0, hp]
        zero = jnp.zeros_like(q)
        q2_ref[hp] = jnp.concatenate([jnp.where(head0, q, zero), jnp.where(head0, zero, q)], axis=0)
    acc_ref[...] = jnp.zeros_like(acc_ref)
    c_ref[...] = jnp.zeros_like(c_ref)

    assert n_pairs % 2 == 0
    per_stream = n_pairs // 2

    def advance(hp, i):
        i = i + 1
        wrap = (i == n_kt).astype(jnp.int32)
        hp = hp + wrap
        i = i * (1 - wrap)
        done = (hp == per_stream).astype(jnp.int32)
        return hp - done, i + done * (n_kt - 1)

    def rows_of(ref, hp, i):
        return ref[0, hp, pl.ds(pl.multiple_of((qi - i) * tile, tile), tile), :]

    def scores(x, hp, i):
        hp = x * per_stream + hp
        z_ref[x] = _dot_nt(q2_ref[hp], rows_of(k_ref, hp, i)) + off_ref[hp, (i == 0).astype(jnp.int32)]

    def gates(x):
        z = z_ref[x]
        log1m = -_softplus_logits(z)
        lb_ref[x] = z + log1m
        tail_ref[x] = _dot(log1m, suffix_ref[...])
        rs_ref[x] = jnp.sum(log1m, axis=-1, keepdims=True)

    def weights(x, hp):
        hp = x * per_stream + hp
        w_ref[x] = jnp.exp(lb_ref[x] + tail_ref[x] + c_ref[hp]).astype(_BF16)
        c_ref[hp] += rs_ref[x]

    def absorb(x, hp, i):
        hp = x * per_stream + hp
        acc_ref[hp] += _dot(w_ref[x], rows_of(v_ref, hp, i))

    zero = jnp.int32(0)
    t0 = (zero, zero)
    t1 = advance(*t0)
    t2 = advance(*t1)
    for x in range(2):
        scores(x, *t0)
    for x in range(2):
        gates(x)
        scores(x, *t1)
    for x in range(2):
        weights(x, t0[0])
        gates(x)
        scores(x, *t2)

    def step(_, tiles):
        oldest, older, newest = tiles
        nxt = advance(*newest)
        for x in range(2):
            absorb(x, *oldest)
            weights(x, older[0])
            gates(x)
            scores(x, *nxt)
        return older, newest, nxt

    lax.fori_loop(0, per_stream * n_kt, step, (t0, t1, t2))
    for hp in range(n_pairs):
        acc = acc_ref[hp]
        o_ref[0, :, hp * PAIR:(hp + 1) * PAIR] = _head_pair_norm(
            jnp.where(head0, acc[:tile], acc[tile:]), g_ref[hp], head0)


def _attn_prompt(q16, k16, v16, sb_bias, norm_attn):
    b, n_pairs, t, _ = q16.shape
    tile = min(ATTN_TILE, t)
    stage = lambda dtype: pltpu.VMEM((2, 2 * tile, tile), dtype)
    return pl.pallas_call(
        functools.partial(_attn_prompt_kernel, tile=tile, n_pairs=n_pairs),
        out_shape=jax.ShapeDtypeStruct((b, t, n_pairs * PAIR), _F32),
        grid=(b, t // tile),
        in_specs=[pl.BlockSpec(memory_space=pltpu.SMEM),
                  pl.BlockSpec((1, n_pairs, tile, PAIR), lambda bi, qi: (bi, 0, qi, 0)),
                  pl.BlockSpec((1, n_pairs, t, PAIR), lambda bi, qi: (bi, 0, 0, 0)),
                  pl.BlockSpec((1, n_pairs, t, PAIR), lambda bi, qi: (bi, 0, 0, 0)),
                  pl.BlockSpec((n_pairs, 1, PAIR), lambda bi, qi: (0, 0, 0))],
        out_specs=pl.BlockSpec((1, tile, n_pairs * PAIR), lambda bi, qi: (bi, qi, 0)),
        scratch_shapes=[pltpu.VMEM((n_pairs, 2, 2 * tile, tile), _F32),
                        pltpu.VMEM((tile, tile), _BF16),
                        pltpu.VMEM((n_pairs, 2 * tile, PAIR), _BF16),
                        stage(_F32),
                        stage(_F32),
                        stage(_F32),
                        pltpu.VMEM((2, 2 * tile, 1), _F32),
                        stage(_BF16),
                        pltpu.VMEM((n_pairs, 2 * tile, PAIR), _F32),
                        pltpu.VMEM((n_pairs, 2 * tile, 1), _F32)],
        compiler_params=pltpu.CompilerParams(dimension_semantics=("arbitrary", "arbitrary"),
                                             vmem_limit_bytes=ATTN_VMEM_LIMIT),
        name="attn_prompt",
    )(sb_bias, q16, k16, v16, norm_attn.reshape(n_pairs, 1, PAIR))


def _split3(x):
    hi = x.astype(_BF16)
    rest = x - hi.astype(_F32)
    mid = rest.astype(_BF16)
    return hi, mid, (rest - mid.astype(_F32)).astype(_BF16)


def _attn_sample_kernel(pt_ref, q_ref, knt_ref, vn_ref, bias_ref, g_ref, *rest, n_heads, t_new, pages):
    kt_refs = rest[:pages]
    vt_refs = rest[pages:2 * pages]
    o_ref, qbd_ref, acc_ref, c_ref = rest[2 * pages:]
    step = pl.program_id(1)
    rows = n_heads * t_new
    width = n_heads * HEAD_DIM
    r_id = lax.broadcasted_iota(jnp.int32, (rows, width), 0)
    l_id = lax.broadcasted_iota(jnp.int32, (rows, width), 1)
    own_head = _div_pow2(l_id, HEAD_DIM) == _div_pow2(r_id, t_new)
    bias = bias_ref[...]

    def suffix_matrix(n, dtype):
        return (lax.broadcasted_iota(jnp.int32, (n, n), 0) > lax.broadcasted_iota(jnp.int32, (n, n), 1)).astype(dtype)

    @pl.when(step == 0)
    def _():
        rep_r = lax.broadcasted_iota(jnp.int32, (rows, t_new), 0)
        rep_c = lax.broadcasted_iota(jnp.int32, (rows, t_new), 1)
        replicate = (_mod_pow2(rep_r, t_new) == rep_c).astype(_F32)
        q_rep = _dot(replicate, q_ref[0])
        qbd = jnp.where(own_head, q_rep, 0.0).astype(_BF16)
        qbd_ref[...] = qbd
        n_pad = knt_ref.shape[2]
        kr = _mod_pow2(lax.broadcasted_iota(jnp.int32, (rows, n_pad), 0), t_new)
        kc = lax.broadcasted_iota(jnp.int32, (rows, n_pad), 1)
        visible = kc < kr
        z = _dot(qbd, knt_ref[0]) + bias
        log1m = jnp.where(visible, -_softplus_logits(z), 0.0)
        tail = _dot_f32(log1m, suffix_matrix(n_pad, _F32))
        w = jnp.where(visible, jnp.exp(z + log1m + tail), 0.0)
        acc_ref[...] = _dot(w, vn_ref[0])
        c_ref[...] = jnp.sum(log1m, axis=-1, keepdims=True)

    qbd = qbd_ref[...]
    suffix = suffix_matrix(PAGE_SIZE, _BF16)
    zs = [_dot(qbd, kt_refs[j][0]) + bias for j in range(pages)]
    log1ms = [-_softplus_logits(z) for z in zs]
    tails = [sum(_dot(piece, suffix) for piece in _split3(l)) for l in log1ms]
    sums = [jnp.sum(l, axis=-1, keepdims=True) for l in log1ms]
    c = c_ref[...]
    acc = acc_ref[...]
    for j in range(pages):
        w = jnp.exp(zs[j] + log1ms[j] + tails[j] + c)
        acc = acc + _dot_nt(w, vt_refs[j][0])
        c = c + sums[j]
    acc_ref[...] = acc
    c_ref[...] = c

    @pl.when(step == pl.num_programs(1) - 1)
    def _():
        am = jnp.where(own_head, acc_ref[...], 0.0)
        ms = jnp.sum(am * am, axis=-1, keepdims=True) * (1.0 / HEAD_DIM)
        an = am * lax.rsqrt(ms + RMS_EPS)
        col_r = lax.broadcasted_iota(jnp.int32, (t_new, rows), 0)
        col_c = lax.broadcasted_iota(jnp.int32, (t_new, rows), 1)
        collect = (_mod_pow2(col_c, t_new) == col_r).astype(_F32)
        o_ref[0] = _dot_f32(collect, an) * g_ref[...]


def _attn_sample(q, k_new, v_new, cache_k, cache_v, page_table, sb_bias, norm_attn):
    s, t_new, width = q.shape
    n_heads = width // HEAD_DIM
    n_pages = page_table.shape[1]
    pages = min(PAGES_PER_STEP, n_pages)
    steps = n_pages // pages
    n_phys = cache_k.shape[0]
    ckt = cache_k.transpose(0, 2, 3, 1).reshape(n_phys, width, PAGE_SIZE)
    cvt = cache_v.transpose(0, 2, 3, 1).reshape(n_phys, width, PAGE_SIZE)
    n_pad = V7X_SUBLANES
    pad = ((0, 0), (0, n_pad - t_new), (0, 0))
    knt = jnp.pad(k_new, pad).transpose(0, 2, 1)
    vn = jnp.pad(v_new, pad)
    bias_rows = jnp.repeat(sb_bias, t_new).reshape(n_heads * t_new, 1)
    rows = n_heads * t_new

    def page_spec(j):
        return pl.BlockSpec((1, width, PAGE_SIZE),
                            lambda b, st, pt: (pt[b, n_pages - 1 - (st * pages + j)], 0, 0))

    seq_spec = lambda n: pl.BlockSpec((1, n, width), lambda b, st, pt: (b, 0, 0))
    grid_spec = pltpu.PrefetchScalarGridSpec(
        num_scalar_prefetch=1,
        grid=(s, steps),
        in_specs=[seq_spec(t_new),
                  pl.BlockSpec((1, width, n_pad), lambda b, st, pt: (b, 0, 0)),
                  seq_spec(n_pad),
                  pl.BlockSpec((rows, 1), lambda b, st, pt: (0, 0)),
                  pl.BlockSpec((1, width), lambda b, st, pt: (0, 0))]
                 + [page_spec(j) for j in range(pages)] * 2,
        out_specs=seq_spec(t_new),
        scratch_shapes=[pltpu.VMEM((rows, width), _BF16), pltpu.VMEM((rows, width), _F32),
                        pltpu.VMEM((rows, 1), _F32)],
    )
    return pl.pallas_call(
        functools.partial(_attn_sample_kernel, n_heads=n_heads, t_new=t_new, pages=pages),
        out_shape=jax.ShapeDtypeStruct((s, t_new, width), _F32),
        grid_spec=grid_spec,
        compiler_params=_params("parallel", "arbitrary"),
        name="attn_sample",
    )(page_table, q, knt, vn, bias_rows, norm_attn.reshape(1, width),
      *([ckt] * pages), *([cvt] * pages))


def _scan_rows(a, b, h_prev):
    n = a.shape[0]
    row = lax.broadcasted_iota(jnp.int32, (n, 1), 0)
    if n <= V7X_SUBLANES:
        h = jnp.zeros_like(b)
        for t in range(n):
            h_prev = a[t:t + 1] * h_prev + b[t:t + 1]
            h = jnp.where(row == t, h_prev, h)
        return h
    d = 1
    while d < n:
        keep = row >= d
        a_prev = jnp.where(keep, pltpu.roll(a, d, 0), 1.0)
        b_prev = jnp.where(keep, pltpu.roll(b, d, 0), 0.0)
        b = b + a * b_prev
        a = a * a_prev
        d *= 2
    return b + a * h_prev


def _rglru_kernel(xr_ref, yb_ref, conv0_ref, h0_ref, cw_ref, cb_ref, wa_ref, ba_ref, wx_ref, bx_ref,
                  lam_ref, g_ref, rg_ref, conv_out_ref, h_out_ref, xbuf_ref, h_ref, *, tt):
    i = pl.program_id(1)
    hist = CONV_WIDTH - 1
    base = V7X_SUBLANES

    @pl.when(i == 0)
    def _():
        xbuf_ref[base - hist:base, :] = conv0_ref[0]
        h_ref[...] = h0_ref[0]

    xbuf_ref[base:base + tt, :] = xr_ref[0]
    xc = cb_ref[...]
    for j in range(CONV_WIDTH):
        xc = xc + cw_ref[j:j + 1, :] * xbuf_ref[base - hist + j:base - hist + j + tt, :]
    tail = xbuf_ref[base + tt - hist:base + tt, :]
    conv_out_ref[0] = tail
    xbuf_ref[base - hist:base, :] = tail

    r = jax.nn.sigmoid(_dot(xc, wa_ref[...]) + ba_ref[...])
    gate_i = jax.nn.sigmoid(_dot(xc, wx_ref[...]) + bx_ref[...])
    log_a = -RG_C * r * _softplus(-lam_ref[...])
    a = jnp.exp(log_a)
    b_in = jnp.sqrt(-jnp.tanh(log_a) * (a * a + 1.0)) * (gate_i * xc)
    h = _scan_rows(a, b_in, h_ref[...])
    h_last = h[tt - 1:tt, :]
    h_ref[...] = h_last
    h_out_ref[0] = h_last
    rg_ref[0] = _rms(h * jax.nn.gelu(yb_ref[0]), g_ref[...])


def _block_diag(w):
    n, r, c = w.shape
    eye = jnp.eye(n, dtype=w.dtype)
    return (eye[:, None, :, None] * w[:, :, None, :]).reshape(n * r, n * c)


def _rglru(xr, yb, conv0, h0, p, *, tt):
    b, t, c = xr.shape
    hist = CONV_WIDTH - 1
    wa = _block_diag(p["rg_wa"]).astype(_BF16)
    wx = _block_diag(p["rg_wx"]).astype(_BF16)
    row = lambda v: v.reshape(1, c)
    tile = pl.BlockSpec((1, tt, c), lambda bi, i: (bi, i, 0))
    const = lambda shape: pl.BlockSpec(shape, lambda bi, i: (0,) * len(shape))
    return pl.pallas_call(
        functools.partial(_rglru_kernel, tt=tt),
        out_shape=[jax.ShapeDtypeStruct((b, t, c), _F32),
                   jax.ShapeDtypeStruct((b, hist, c), _F32),
                   jax.ShapeDtypeStruct((b, 1, c), _F32)],
        grid=(b, t // tt),
        in_specs=[tile, tile,
                  pl.BlockSpec((1, hist, c), lambda bi, i: (bi, 0, 0)),
                  pl.BlockSpec((1, 1, c), lambda bi, i: (bi, 0, 0)),
                  const((CONV_WIDTH, c)), const((1, c)), const((c, c)), const((1, c)),
                  const((c, c)), const((1, c)), const((1, c)), const((1, c))],
        out_specs=[tile,
                   pl.BlockSpec((1, hist, c), lambda bi, i: (bi, 0, 0)),
                   pl.BlockSpec((1, 1, c), lambda bi, i: (bi, 0, 0))],
        scratch_shapes=[pltpu.VMEM((V7X_SUBLANES + tt, c), _F32), pltpu.VMEM((1, c), _F32)],
        compiler_params=_params("parallel", "arbitrary"),
        name="rglru",
    )(xr, yb, conv0, h0.reshape(b, 1, c), p["conv_w"], row(p["conv_b"]), wa, row(p["rg_ba"]),
      wx, row(p["rg_bx"]), row(p["rg_lambda"]), row(p["norm_rg"]))


ROUTE_EXPERT, ROUTE_GATE, ROUTE_RANK = 0, TOP_K, 2 * TOP_K


def _route(logits, counts):
    lane = lax.broadcasted_iota(jnp.int32, logits.shape, 1)
    lane_f = lane.astype(_F32)
    far = float(V7X_LANES)
    neg = -jnp.inf
    is_group = (lane >= N_EXPERTS) & (lane < N_EXPERTS + N_GROUPS)
    g_logit = jnp.where(is_group, logits, neg)
    g_max = jnp.max(g_logit, axis=-1, keepdims=True)
    grp = jnp.min(jnp.where(g_logit == g_max, lane_f - N_EXPERTS, far), axis=-1, keepdims=True)
    p_sel = 1.0 / jnp.sum(jnp.where(is_group, jnp.exp(logits - g_max), 0.0), axis=-1, keepdims=True)
    in_group = _div_pow2(lane, EXPERTS_PER_GROUP).astype(_F32) == grp
    e_logit = jnp.where(in_group, logits, neg)
    v1 = jnp.max(e_logit, axis=-1, keepdims=True)
    i1 = jnp.min(jnp.where(e_logit == v1, lane_f, far), axis=-1, keepdims=True)
    e_rest = jnp.where(lane_f == i1, neg, e_logit)
    v2 = jnp.max(e_rest, axis=-1, keepdims=True)
    i2 = jnp.min(jnp.where(e_rest == v2, lane_f, far), axis=-1, keepdims=True)
    t = jnp.exp(v2 - v1)
    g1 = p_sel / (1.0 + t)
    g2 = p_sel * t / (1.0 + t)
    n = logits.shape[0]
    chosen = jnp.where(lane_f == i1, 1.0, jnp.where(lane_f == i2, 1.0, 0.0))
    earlier = (lax.broadcasted_iota(jnp.int32, (n, n), 0) > lax.broadcasted_iota(jnp.int32, (n, n), 1))
    before = _dot(earlier.astype(_BF16), chosen) + counts
    r1 = jnp.sum(jnp.where(lane_f == i1, before, 0.0), axis=-1, keepdims=True)
    r2 = jnp.sum(jnp.where(lane_f == i2, before, 0.0), axis=-1, keepdims=True)
    route = jnp.zeros_like(logits)
    for offset, values in ((ROUTE_EXPERT, (i1, i2)), (ROUTE_GATE, (g1, g2)), (ROUTE_RANK, (r1, r2))):
        for k, val in enumerate(values):
            route = jnp.where(lane == offset + k, val, route)
    return route, counts + jnp.sum(chosen, axis=0, keepdims=True)


def _outproj_kernel(x_ref, attn_ref, rg_ref, gate_ref, scale_ref, shift_ref, wo_a_ref, wo_r_ref,
                    g_ref, wr_ref, br_ref, x2_ref, xn2_ref, route_ref, counts_ref):
    @pl.when((pl.program_id(0) == 0) & (pl.program_id(1) == 0))
    def _():
        counts_ref[...] = jnp.zeros_like(counts_ref)

    mix = _dot(attn_ref[0], wo_a_ref[...]) + _dot(rg_ref[0], wo_r_ref[...])
    x2 = x_ref[0] + gate_ref[0] * mix
    xn2 = _rms(x2, g_ref[...]) * (1.0 + scale_ref[0]) + shift_ref[0]
    x2_ref[0] = x2
    xn2_ref[0] = xn2
    route_ref[0], counts_ref[...] = _route(_dot(xn2, wr_ref[...]) + br_ref[...], counts_ref[...])


def _outproj(x, attn, rg, gate, scale, shift, p, *, tm):
    g, t, d = x.shape
    w_attn = attn.shape[-1]
    wdt = _BF16
    wo_a = p["w_out"][:w_attn].astype(wdt)
    wo_r = p["w_out"][w_attn:].astype(wdt)
    pad = V7X_LANES - N_EXPERTS - N_GROUPS
    w_route = jnp.pad(jnp.concatenate([p["w_router"], p["w_group"]], axis=1), ((0, 0), (0, pad))).astype(wdt)
    b_route = jnp.pad(jnp.concatenate([p["b_router"], p["b_group"]]), (0, pad)).reshape(1, V7X_LANES)
    mod_rows = gate.shape[1]
    mod_block = (1, 1, d) if mod_rows == 1 else (1, tm, d)
    mod_map = (lambda b, i: (b, 0, 0)) if mod_rows == 1 else (lambda b, i: (b, i, 0))
    mod_spec = pl.BlockSpec(mod_block, mod_map)
    tok = lambda w: pl.BlockSpec((1, tm, w), lambda b, i: (b, i, 0))
    const = lambda shape: pl.BlockSpec(shape, lambda b, i: (0,) * len(shape))
    return pl.pallas_call(
        _outproj_kernel,
        out_shape=[jax.ShapeDtypeStruct((g, t, d), _F32), jax.ShapeDtypeStruct((g, t, d), _F32),
                   jax.ShapeDtypeStruct((g, t, V7X_LANES), _F32), jax.ShapeDtypeStruct((1, V7X_LANES), _F32)],
        grid=(g, t // tm),
        in_specs=[tok(d), tok(w_attn), tok(rg.shape[-1]), mod_spec, mod_spec, mod_spec,
                  const(wo_a.shape), const(wo_r.shape), const((1, d)), const(w_route.shape),
                  const((1, V7X_LANES))],
        out_specs=[tok(d), tok(d), tok(V7X_LANES), const((1, V7X_LANES))],
        compiler_params=_params("arbitrary", "arbitrary"),
        name="outproj",
    )(x, attn, rg, gate, scale, shift, wo_a, wo_r, p["norm_ffn"].reshape(1, d), w_route, b_route)


def _moe_ffn_kernel(be_ref, nvalid_ref, x_ref, w1_ref, w3_ref, w2_ref, y_ref, w1b_ref, w3b_ref, w2b_ref):
    i = pl.program_id(0)
    new_expert = (i == 0) | (be_ref[i] != be_ref[jnp.maximum(i - 1, 0)])

    @pl.when(nvalid_ref[i] > 0)
    def _():
        @pl.when(new_expert)
        def _():
            w1b_ref[...] = w1_ref[0].astype(_BF16)
            w3b_ref[...] = w3_ref[0].astype(_BF16)
            w2b_ref[...] = w2_ref[0].astype(_BF16)

        xb = x_ref[...].astype(_BF16)
        up = _dot(xb, w1b_ref[...])
        hidden = up * jax.nn.sigmoid(up) * _dot(xb, w3b_ref[...])
        y_ref[...] = _dot(hidden, w2b_ref[...])

    @pl.when(nvalid_ref[i] == 0)
    def _():
        y_ref[...] = jnp.zeros_like(y_ref)


def _moe_layout(counts, n_blocks, block):
    padded = (counts + block - 1) // block * block
    pad_end = jnp.cumsum(padded)
    pad_start = pad_end - padded
    block_start = jnp.arange(n_blocks, dtype=jnp.int32) * block
    block_expert = jnp.minimum(jnp.searchsorted(pad_end, block_start, side="right"), N_EXPERTS - 1).astype(jnp.int32)
    n_valid = jnp.clip(pad_start[block_expert] + counts[block_expert] - block_start, 0, block).astype(jnp.int32)
    return pad_start, block_expert, n_valid


def _slot_rows(route, first_row):
    experts = route[:, ROUTE_EXPERT:ROUTE_EXPERT + TOP_K].astype(jnp.int32)
    rank = route[:, ROUTE_RANK:ROUTE_RANK + TOP_K].astype(jnp.int32)
    return first_row[experts] + rank


COPY_UNROLL = 16


def _for_each_copy(n, fn):
    assert n % COPY_UNROLL == 0
    lax.fori_loop(0, n, lambda j, _: (fn(j), 0)[1], 0, unroll=COPY_UNROLL)


def _dispatch_kernel(dest_ref, x_ref, rows_in_hbm, rows_hbm, sem_ref):
    del rows_in_hbm
    tm = x_ref.shape[1]

    def copy(j):
        return pltpu.make_async_copy(x_ref.at[0, pl.ds(_div_pow2(j, TOP_K), 1)],
                                     rows_hbm.at[pl.ds(dest_ref[0, 0, j], 1)], sem_ref.at[0])

    _for_each_copy(tm * TOP_K, lambda j: copy(j).start())
    _for_each_copy(tm * TOP_K, lambda j: copy(j).wait())


def _dispatch(xn2, dest, rows, *, tm):
    g, t, d = xn2.shape
    steps = t // tm
    dest_blocks = dest.reshape(g * steps, 1, tm * TOP_K)
    return pl.pallas_call(
        _dispatch_kernel,
        out_shape=jax.ShapeDtypeStruct(rows.shape, rows.dtype),
        grid=(g, steps),
        in_specs=[pl.BlockSpec((1, 1, tm * TOP_K), lambda b, i: (b * steps + i, 0, 0), memory_space=pltpu.SMEM),
                  pl.BlockSpec((1, tm, d), lambda b, i: (b, i, 0)),
                  pl.BlockSpec(memory_space=pl.ANY)],
        out_specs=pl.BlockSpec(memory_space=pl.ANY),
        scratch_shapes=[pltpu.SemaphoreType.DMA((1,))],
        input_output_aliases={2: 0},
        compiler_params=_params("arbitrary", "arbitrary"),
        name="moe_dispatch",
    )(dest_blocks, xn2, rows)


def _moe_ffn(rows, block_expert, n_valid, w1, w3, w2):
    n_rows, d = rows.shape
    d_exp = w1.shape[-1]
    block = MOE_BLOCK
    row_spec = pl.BlockSpec((block, d), lambda i, be, nv: (i, 0))
    grid_spec = pltpu.PrefetchScalarGridSpec(
        num_scalar_prefetch=2,
        grid=(n_rows // block,),
        in_specs=[row_spec,
                  pl.BlockSpec((1, d, d_exp), lambda i, be, nv: (be[i], 0, 0)),
                  pl.BlockSpec((1, d, d_exp), lambda i, be, nv: (be[i], 0, 0)),
                  pl.BlockSpec((1, d_exp, d), lambda i, be, nv: (be[i], 0, 0))],
        out_specs=row_spec,
        scratch_shapes=[pltpu.VMEM((d, d_exp), _BF16), pltpu.VMEM((d, d_exp), _BF16),
                        pltpu.VMEM((d_exp, d), _BF16)],
    )
    return pl.pallas_call(
        _moe_ffn_kernel,
        out_shape=jax.ShapeDtypeStruct((n_rows, d), _F32),
        grid_spec=grid_spec,
        compiler_params=_params("arbitrary"),
        name="moe_ffn",
    )(block_expert, n_valid, rows, w1, w3, w2)


def _final_kernel(dest_ref, x2_ref, route_ref, gate_ref, g_ref, y_hbm, o_ref, ybuf_ref, sem_ref):
    tm = x2_ref.shape[1]

    def copy(j):
        return pltpu.make_async_copy(y_hbm.at[pl.ds(dest_ref[0, 0, j], 1)],
                                     ybuf_ref.at[_mod_pow2(j, TOP_K), pl.ds(_div_pow2(j, TOP_K), 1)], sem_ref.at[0])

    _for_each_copy(tm * TOP_K, lambda j: copy(j).start())
    _for_each_copy(tm * TOP_K, lambda j: copy(j).wait())
    route = route_ref[0]
    ffn = route[:, ROUTE_GATE:ROUTE_GATE + 1] * ybuf_ref[0]
    for k in range(1, TOP_K):
        ffn = ffn + route[:, ROUTE_GATE + k:ROUTE_GATE + k + 1] * ybuf_ref[k]
    o_ref[0] = _rms(x2_ref[0] + gate_ref[0] * ffn, g_ref[...])


def _final(x2, route, dest, y_rows, gate, final_norm, *, tm):
    g, t, d = x2.shape
    mod_rows = gate.shape[1]
    mod_block = (1, 1, d) if mod_rows == 1 else (1, tm, d)
    mod_map = (lambda b, i: (b, 0, 0)) if mod_rows == 1 else (lambda b, i: (b, i, 0))
    steps = t // tm
    dest_blocks = dest.reshape(g * steps, 1, tm * TOP_K)
    tok = lambda w: pl.BlockSpec((1, tm, w), lambda b, i: (b, i, 0))
    return pl.pallas_call(
        _final_kernel,
        out_shape=jax.ShapeDtypeStruct((g, t, d), _F32),
        grid=(g, steps),
        in_specs=[pl.BlockSpec((1, 1, tm * TOP_K), lambda b, i: (b * steps + i, 0, 0), memory_space=pltpu.SMEM),
                  tok(d), tok(V7X_LANES),
                  pl.BlockSpec(mod_block, mod_map),
                  pl.BlockSpec((1, d), lambda b, i: (0, 0)),
                  pl.BlockSpec(memory_space=pl.ANY)],
        out_specs=tok(d),
        scratch_shapes=[pltpu.VMEM((TOP_K, tm, d), _F32), pltpu.SemaphoreType.DMA((1,))],
        compiler_params=_params("arbitrary", "arbitrary"),
        name="final",
    )(dest_blocks, x2, route, gate, final_norm.reshape(1, d), y_rows)


def _token_tile(t, want):
    return want if t % want == 0 else t


def kernel(x_prompt, x_sample, cache_k, cache_v, state_conv, state_h, page_table, c_prompt, c_sample,
           w_ada, b_ada, norm_mix, norm_ffn, w_in, sb_bias, norm_attn, conv_w, conv_b, rg_wa, rg_ba,
           rg_wx, rg_bx, rg_lambda, norm_rg, w_out, w_group, b_group, w_router, b_router, w1, w3, w2,
           final_norm):
    depth = w_ada.shape[0]
    assert depth == 1, "the final RMSNorm is fused into the layer's second residual add"
    bp, tp, d = x_prompt.shape
    bs, ts, _ = x_sample.shape
    n_s = bs * ts
    xp = x_prompt
    xs = x_sample.reshape(1, n_s, d)
    outs = [[] for _ in range(8)]
    for l in range(depth):
        p = dict(w_ada=w_ada[l], b_ada=b_ada[l], norm_mix=norm_mix[l], norm_ffn=norm_ffn[l], w_in=w_in[l],
                 sb_bias=sb_bias[l], norm_attn=norm_attn[l], conv_w=conv_w[l], conv_b=conv_b[l],
                 rg_wa=rg_wa[l], rg_ba=rg_ba[l], rg_wx=rg_wx[l], rg_bx=rg_bx[l], rg_lambda=rg_lambda[l],
                 norm_rg=norm_rg[l], w_out=w_out[l], w_group=w_group[l], b_group=b_group[l],
                 w_router=w_router[l], b_router=b_router[l])
        w_attn = p["w_in"].shape[1] // 5
        n_heads = w_attn // HEAD_DIM
        c_rg = w_attn
        mod = _ada(jnp.concatenate([c_prompt, c_sample], axis=0), p["w_ada"], p["b_ada"])
        mod_p = mod[:bp].reshape(bp, 6, 1, d)
        mod_s = jnp.repeat(mod[bp:].reshape(bs, 6, d), ts, axis=0).reshape(n_s, 6, d)
        shift1_p, scale1_p, gate1_p, shift2_p, scale2_p, gate2_p = (mod_p[:, j] for j in range(6))
        shift1_s, scale1_s, gate1_s, shift2_s, scale2_s, gate2_s = (mod_s[None, :, j] for j in range(6))

        w_in16 = p["w_in"].astype(_BF16)

        tm_p = _token_tile(tp, 512)
        k_p, v_p, xr_p, yb_p, q16, k16, v16 = _inproj(xp, scale1_p, shift1_p, p["norm_mix"], w_in16, tm=tm_p,
                                                      pair_major=True)
        attn_p = _attn_prompt(q16, k16, v16, p["sb_bias"], p["norm_attn"])
        conv0 = jnp.zeros((bp, CONV_WIDTH - 1, c_rg), _F32)
        h0 = jnp.zeros((bp, c_rg), _F32)
        rg_p, conv_p, h_p = _rglru(xr_p, yb_p, conv0, h0, p, tt=_token_tile(tp, 512))
        x2_p, xn2_p, route_p, counts_p = _outproj(xp, attn_p, rg_p, gate1_p, scale2_p, shift2_p, p, tm=tm_p)

        k_s, v_s, xr_s, yb_s, q_s = _inproj(xs, scale1_s, shift1_s, p["norm_mix"], w_in16, tm=n_s,
                                            pair_major=False)
        seq = lambda a: a.reshape(bs, ts, a.shape[-1])
        attn_s = _attn_sample(seq(q_s), seq(k_s), seq(v_s), cache_k[l], cache_v[l], page_table,
                              p["sb_bias"], p["norm_attn"])
        rg_s, conv_s, h_s = _rglru(seq(xr_s), seq(yb_s), state_conv[l], state_h[l], p, tt=ts)
        x2_s, xn2_s, route_s, counts_s = _outproj(xs, attn_s.reshape(1, n_s, w_attn), rg_s.reshape(1, n_s, c_rg),
                                                  gate1_s, scale2_s, shift2_s, p, tm=n_s)

        n_p = bp * tp
        n_blocks = pl.cdiv((n_p + n_s) * TOP_K + N_EXPERTS * (MOE_BLOCK - 1), MOE_BLOCK)
        slots_p = counts_p[0, :N_EXPERTS].astype(jnp.int32)
        slots_s = counts_s[0, :N_EXPERTS].astype(jnp.int32)
        first_row, block_expert, n_valid = _moe_layout(slots_p + slots_s, n_blocks, MOE_BLOCK)
        dest_p = _slot_rows(route_p.reshape(n_p, V7X_LANES), first_row)
        dest_s = _slot_rows(route_s.reshape(n_s, V7X_LANES), first_row + slots_p)
        rows = jnp.zeros((n_blocks * MOE_BLOCK, d), _F32)
        rows = _dispatch(xn2_p, dest_p, rows, tm=tm_p)
        rows = _dispatch(xn2_s, dest_s, rows, tm=n_s)
        y_rows = _moe_ffn(rows, block_expert, n_valid, w1[l], w3[l], w2[l])
        xp = _final(x2_p, route_p, dest_p, y_rows, gate2_p, final_norm, tm=tm_p)
        xs = _final(x2_s, route_s, dest_s, y_rows, gate2_s, final_norm, tm=n_s)
        hd = (n_heads, HEAD_DIM)
        for lst, val in zip(outs, (k_p.reshape(bp, tp, *hd), v_p.reshape(bp, tp, *hd), conv_p,
                                   h_p.reshape(bp, c_rg), seq(k_s).reshape(bs, ts, *hd),
                                   seq(v_s).reshape(bs, ts, *hd), conv_s, h_s.reshape(bs, c_rg))):
            lst.append(val)
    stacked = [jnp.stack(o) for o in outs]
    return (xp, xs.reshape(bs, ts, d), *stacked)
```

```python
import functools

import jax
import jax.numpy as jnp
from jax import lax
from jax.experimental import pallas as pl
from jax.experimental.pallas import tpu as pltpu

HEAD_DIM = 64
N_RG_BLOCKS = 8
CONV_WIDTH = 4
RG_C = 8.0
N_GROUPS = 4
EXPERTS_PER_GROUP = 8
N_EXPERTS = N_GROUPS * EXPERTS_PER_GROUP
TOP_K = 2
RMS_EPS = 1e-6
PAGE_SIZE = 128

V7X_LANES = 128
V7X_SUBLANES = 8
VMEM_LIMIT = 48 * 1024 * 1024
ATTN_VMEM_LIMIT = 56 * 1024 * 1024

MOE_BLOCK = 256
ATTN_TILE = 256
PAGES_PER_STEP = 16
MASKED_LOGIT = -1e30

_F32 = jnp.float32
_BF16 = jnp.bfloat16
_HIGHEST = lax.Precision.HIGHEST


def _dot(a, b):
    return jnp.dot(a.astype(_BF16), b.astype(_BF16), preferred_element_type=_F32)


def _dot_nt(a, b):
    return lax.dot_general(a.astype(_BF16), b.astype(_BF16), (((1,), (1,)), ((), ())),
                           preferred_element_type=_F32)


def _dot_f32(a, b):
    return jnp.dot(a, b, precision=_HIGHEST, preferred_element_type=_F32)


def _rms(x, gain):
    return x * lax.rsqrt(jnp.mean(x * x, axis=-1, keepdims=True) + RMS_EPS) * gain


def _softplus(z):
    return jnp.maximum(z, 0.0) + jnp.log1p(jnp.exp(-jnp.abs(z)))


def _softplus_logits(z):
    return jnp.maximum(z, 0.0) + jnp.log(1.0 + jnp.exp(-jnp.abs(z)))


def _div_pow2(x, n):
    assert n & (n - 1) == 0
    return lax.shift_right_logical(x, n.bit_length() - 1)


def _mod_pow2(x, n):
    assert n & (n - 1) == 0
    return lax.bitwise_and(x, n - 1)


def _params(*sem):
    return pltpu.CompilerParams(dimension_semantics=sem, vmem_limit_bytes=VMEM_LIMIT)


def _ada_kernel(c_ref, w_ref, b_ref, o_ref):
    c = c_ref[...]
    o_ref[...] = _dot(c * jax.nn.sigmoid(c), w_ref[...]) + b_ref[...]


def _ada(c, w_ada, b_ada):
    rows, d = c.shape
    n_chunks = w_ada.shape[1] // d
    return pl.pallas_call(
        _ada_kernel,
        out_shape=jax.ShapeDtypeStruct((rows, n_chunks * d), _F32),
        grid=(n_chunks,),
        in_specs=[pl.BlockSpec((rows, d), lambda j: (0, 0)),
                  pl.BlockSpec((d, d), lambda j: (0, j)),
                  pl.BlockSpec((1, d), lambda j: (0, j))],
        out_specs=pl.BlockSpec((rows, d), lambda j: (0, j)),
        compiler_params=_params("parallel"),
        name="ada",
    )(c, w_ada, b_ada.reshape(1, -1))


PAIR = 2 * HEAD_DIM


def _inproj_kernel(x_ref, scale_ref, shift_ref, g_ref, w_ref, k_ref, v_ref, xr_ref, yb_ref, q16_ref,
                   *kv16_refs, w_attn):
    xn = (_rms(x_ref[0], g_ref[...]) * (1.0 + scale_ref[0]) + shift_ref[0]).astype(_BF16)

    def col(j):
        return _dot(xn, w_ref[:, j * w_attn:(j + 1) * w_attn])

    q16 = (col(0) * (HEAD_DIM ** -0.5)).astype(_BF16)
    k = col(1)
    v = col(2)
    k_ref[0] = k
    v_ref[0] = v
    xr_ref[0] = col(3)
    yb_ref[0] = col(4)
    if kv16_refs:
        for ref, val in zip((q16_ref,) + kv16_refs, (q16, k.astype(_BF16), v.astype(_BF16))):
            for hp in range(w_attn // PAIR):
                ref[0, hp] = val[:, hp * PAIR:(hp + 1) * PAIR]
    else:
        q16_ref[0] = q16


def _inproj(x, scale, shift, gain, w_in, *, tm, pair_major):
    g, t, d = x.shape
    w_attn = w_in.shape[1] // 5
    mod_rows = scale.shape[1]
    mod_block = (1, 1, d) if mod_rows == 1 else (1, tm, d)
    mod_map = (lambda b, i: (b, 0, 0)) if mod_rows == 1 else (lambda b, i: (b, i, 0))
    out_block = pl.BlockSpec((1, tm, w_attn), lambda b, i: (b, i, 0))
    out_shape = [jax.ShapeDtypeStruct((g, t, w_attn), _F32)] * 4
    out_specs = [out_block] * 4
    if pair_major:
        n_pairs = w_attn // PAIR
        out_shape += [jax.ShapeDtypeStruct((g, n_pairs, t, PAIR), _BF16)] * 3
        out_specs += [pl.BlockSpec((1, n_pairs, tm, PAIR), lambda b, i: (b, 0, i, 0))] * 3
    else:
        out_shape += [jax.ShapeDtypeStruct((g, t, w_attn), _BF16)]
        out_specs += [out_block]
    return pl.pallas_call(
        functools.partial(_inproj_kernel, w_attn=w_attn),
        out_shape=out_shape,
        grid=(g, t // tm),
        in_specs=[pl.BlockSpec((1, tm, d), lambda b, i: (b, i, 0)),
                  pl.BlockSpec(mod_block, mod_map),
                  pl.BlockSpec(mod_block, mod_map),
                  pl.BlockSpec((1, d), lambda b, i: (0, 0)),
                  pl.BlockSpec(w_in.shape, lambda b, i: (0, 0))],
        out_specs=out_specs,
        compiler_params=_params("parallel", "parallel"),
        name="inproj",
    )(x, scale, shift, gain.reshape(1, d), w_in)


def _head_pair_norm(acc, gain, head0):
    sq = acc * acc
    s0 = jnp.sum(jnp.where(head0, sq, 0.0), axis=-1, keepdims=True)
    s1 = jnp.sum(jnp.where(head0, 0.0, sq), axis=-1, keepdims=True)
    ms = jnp.where(head0, s0, s1) * (1.0 / HEAD_DIM)
    return acc * lax.rsqrt(ms + RMS_EPS) * gain


def _attn_prompt_kernel(bias_ref, q_ref, k_ref, v_ref, g_ref, o_ref, off_ref, suffix_ref, q2_ref,
                        z_ref, lb_ref, tail_ref, rs_ref, w_ref, acc_ref, c_ref, *, tile, n_pairs, q_tiles):
    g = pl.program_id(1)
    lane = lax.broadcasted_iota(jnp.int32, (1, PAIR), 1)
    head0 = lane < HEAD_DIM

    @pl.when((pl.program_id(0) == 0) & (g == 0))
    def _():
        row = lax.broadcasted_iota(jnp.int32, (tile, tile), 0)
        col = lax.broadcasted_iota(jnp.int32, (tile, tile), 1)
        suffix_ref[...] = (row > col).astype(_BF16)
        row2 = lax.broadcasted_iota(jnp.int32, (2 * tile, tile), 0)
        col2 = lax.broadcasted_iota(jnp.int32, (2 * tile, tile), 1)
        causal = col2 < _mod_pow2(row2, tile)
        first = lax.broadcasted_iota(jnp.int32, (2 * tile, 1), 0) < tile
        for hp in range(n_pairs):
            bias = jnp.where(first, bias_ref[2 * hp], bias_ref[2 * hp + 1])
            off_ref[hp, 0] = jnp.broadcast_to(bias, (2 * tile, tile))
            off_ref[hp, 1] = jnp.where(causal, bias, MASKED_LOGIT)

    for qt in range(q_tiles):
        for hp in range(n_pairs):
            q = q_ref[0, hp, qt * tile:(qt + 1) * tile, :]
            zero = jnp.zeros_like(q)
            q2_ref[qt * n_pairs + hp] = jnp.concatenate([jnp.where(head0, q, zero), jnp.where(head0, zero, q)],
                                                        axis=0)
    acc_ref[...] = jnp.zeros_like(acc_ref)
    c_ref[...] = jnp.zeros_like(c_ref)

    assert n_pairs % 2 == 0
    per_stream = n_pairs // 2
    n_seg = q_tiles * per_stream

    def q_tile_of(seg):
        return q_tiles * g + _div_pow2(seg, per_stream)

    def advance(seg, i):
        i = i + 1
        wrap = (i == q_tile_of(seg) + 1).astype(jnp.int32)
        seg = seg + wrap
        i = i * (1 - wrap)
        done = (seg == n_seg).astype(jnp.int32)
        seg = seg - done
        return seg, i + done * q_tile_of(seg)

    def pair_of(x, seg):
        hp = x * per_stream + _mod_pow2(seg, per_stream)
        return hp, _div_pow2(seg, per_stream) * n_pairs + hp

    def rows_of(ref, hp, seg, i):
        return ref[0, hp, pl.ds(pl.multiple_of((q_tile_of(seg) - i) * tile, tile), tile), :]

    def scores(x, seg, i):
        hp, state = pair_of(x, seg)
        z_ref[x] = (_dot_nt(q2_ref[state], rows_of(k_ref, hp, seg, i))
                    + off_ref[hp, (i == 0).astype(jnp.int32)])

    def gates(x):
        z = z_ref[x]
        neg_z = -z
        log1m = jnp.minimum(neg_z, 0.0) - jnp.log(1.0 + jnp.exp(jnp.minimum(z, neg_z)))
        lb_ref[x] = z + log1m
        tail_ref[x] = _dot(log1m, suffix_ref[...])
        rs_ref[x] = jnp.sum(log1m, axis=-1, keepdims=True)

    def weights(x, seg):
        _, state = pair_of(x, seg)
        w_ref[x] = jnp.exp(lb_ref[x] + tail_ref[x] + c_ref[state]).astype(_BF16)
        c_ref[state] += rs_ref[x]

    def absorb(x, seg, i):
        hp, state = pair_of(x, seg)
        acc_ref[state] += _dot(w_ref[x], rows_of(v_ref, hp, seg, i))

    zero = jnp.int32(0)
    t0 = (zero, zero)
    t1 = advance(*t0)
    t2 = advance(*t1)
    for x in range(2):
        scores(x, *t0)
    for x in range(2):
        gates(x)
        scores(x, *t1)
    for x in range(2):
        weights(x, t0[0])
        gates(x)
        scores(x, *t2)

    def step(_, tiles):
        oldest, older, newest = tiles
        nxt = advance(*newest)
        for x in range(2):
            absorb(x, *oldest)
            weights(x, older[0])
            gates(x)
            scores(x, *nxt)
        return older, newest, nxt

    n_steps = per_stream * sum(q_tiles * g + qt + 1 for qt in range(q_tiles))
    lax.fori_loop(0, n_steps, step, (t0, t1, t2))
    for qt in range(q_tiles):
        for hp in range(n_pairs):
            acc = acc_ref[qt * n_pairs + hp]
            o_ref[0, qt * tile:(qt + 1) * tile, hp * PAIR:(hp + 1) * PAIR] = _head_pair_norm(
                jnp.where(head0, acc[:tile], acc[tile:]), g_ref[hp], head0)


def _attn_prompt(q16, k16, v16, sb_bias, norm_attn):
    b, n_pairs, t, _ = q16.shape
    tile = min(ATTN_TILE, t)
    q_tiles = 2 if (t // tile) % 2 == 0 else 1
    stage = lambda dtype: pltpu.VMEM((2, 2 * tile, tile), dtype)
    return pl.pallas_call(
        functools.partial(_attn_prompt_kernel, tile=tile, n_pairs=n_pairs, q_tiles=q_tiles),
        out_shape=jax.ShapeDtypeStruct((b, t, n_pairs * PAIR), _F32),
        grid=(b, t // (tile * q_tiles)),
        in_specs=[pl.BlockSpec(memory_space=pltpu.SMEM),
                  pl.BlockSpec((1, n_pairs, q_tiles * tile, PAIR), lambda bi, gi: (bi, 0, gi, 0)),
                  pl.BlockSpec((1, n_pairs, t, PAIR), lambda bi, gi: (bi, 0, 0, 0)),
                  pl.BlockSpec((1, n_pairs, t, PAIR), lambda bi, gi: (bi, 0, 0, 0)),
                  pl.BlockSpec((n_pairs, 1, PAIR), lambda bi, gi: (0, 0, 0))],
        out_specs=pl.BlockSpec((1, q_tiles * tile, n_pairs * PAIR), lambda bi, gi: (bi, gi, 0)),
        scratch_shapes=[pltpu.VMEM((n_pairs, 2, 2 * tile, tile), _F32),
                        pltpu.VMEM((tile, tile), _BF16),
                        pltpu.VMEM((q_tiles * n_pairs, 2 * tile, PAIR), _BF16),
                        stage(_F32),
                        stage(_F32),
                        stage(_F32),
                        pltpu.VMEM((2, 2 * tile, 1), _F32),
                        stage(_BF16),
                        pltpu.VMEM((q_tiles * n_pairs, 2 * tile, PAIR), _F32),
                        pltpu.VMEM((q_tiles * n_pairs, 2 * tile, 1), _F32)],
        compiler_params=pltpu.CompilerParams(dimension_semantics=("arbitrary", "arbitrary"),
                                             vmem_limit_bytes=ATTN_VMEM_LIMIT),
        name="attn_prompt",
    )(sb_bias, q16, k16, v16, norm_attn.reshape(n_pairs, 1, PAIR))


def _split3(x):
    hi = x.astype(_BF16)
    rest = x - hi.astype(_F32)
    mid = rest.astype(_BF16)
    return hi, mid, (rest - mid.astype(_F32)).astype(_BF16)


def _attn_sample_kernel(pt_ref, q_ref, knt_ref, vn_ref, bias_ref, g_ref, *rest, n_heads, t_new, pages):
    kt_refs = rest[:pages]
    vt_refs = rest[pages:2 * pages]
    o_ref, qbd_ref, acc_ref, c_ref = rest[2 * pages:]
    step = pl.program_id(1)
    rows = n_heads * t_new
    width = n_heads * HEAD_DIM
    r_id = lax.broadcasted_iota(jnp.int32, (rows, width), 0)
    l_id = lax.broadcasted_iota(jnp.int32, (rows, width), 1)
    own_head = _div_pow2(l_id, HEAD_DIM) == _div_pow2(r_id, t_new)
    bias = bias_ref[...]

    def suffix_matrix(n, dtype):
        return (lax.broadcasted_iota(jnp.int32, (n, n), 0) > lax.broadcasted_iota(jnp.int32, (n, n), 1)).astype(dtype)

    @pl.when(step == 0)
    def _():
        rep_r = lax.broadcasted_iota(jnp.int32, (rows, t_new), 0)
        rep_c = lax.broadcasted_iota(jnp.int32, (rows, t_new), 1)
        replicate = (_mod_pow2(rep_r, t_new) == rep_c).astype(_F32)
        q_rep = _dot(replicate, q_ref[0])
        qbd = jnp.where(own_head, q_rep, 0.0).astype(_BF16)
        qbd_ref[...] = qbd
        n_pad = knt_ref.shape[2]
        kr = _mod_pow2(lax.broadcasted_iota(jnp.int32, (rows, n_pad), 0), t_new)
        kc = lax.broadcasted_iota(jnp.int32, (rows, n_pad), 1)
        visible = kc < kr
        z = _dot(qbd, knt_ref[0]) + bias
        log1m = jnp.where(visible, -_softplus_logits(z), 0.0)
        tail = _dot_f32(log1m, suffix_matrix(n_pad, _F32))
        w = jnp.where(visible, jnp.exp(z + log1m + tail), 0.0)
        acc_ref[...] = _dot(w, vn_ref[0])
        c_ref[...] = jnp.sum(log1m, axis=-1, keepdims=True)

    qbd = qbd_ref[...]
    suffix = suffix_matrix(PAGE_SIZE, _BF16)
    zs = [_dot(qbd, kt_refs[j][0]) + bias for j in range(pages)]
    log1ms = [-_softplus_logits(z) for z in zs]
    tails = [sum(_dot(piece, suffix) for piece in _split3(l)) for l in log1ms]
    sums = [jnp.sum(l, axis=-1, keepdims=True) for l in log1ms]
    c = c_ref[...]
    acc = acc_ref[...]
    for j in range(pages):
        w = jnp.exp(zs[j] + log1ms[j] + tails[j] + c)
        acc = acc + _dot_nt(w, vt_refs[j][0])
        c = c + sums[j]
    acc_ref[...] = acc
    c_ref[...] = c

    @pl.when(step == pl.num_programs(1) - 1)
    def _():
        am = jnp.where(own_head, acc_ref[...], 0.0)
        ms = jnp.sum(am * am, axis=-1, keepdims=True) * (1.0 / HEAD_DIM)
        an = am * lax.rsqrt(ms + RMS_EPS)
        col_r = lax.broadcasted_iota(jnp.int32, (t_new, rows), 0)
        col_c = lax.broadcasted_iota(jnp.int32, (t_new, rows), 1)
        collect = (_mod_pow2(col_c, t_new) == col_r).astype(_F32)
        o_ref[0] = _dot_f32(collect, an) * g_ref[...]


def _attn_sample(q, k_new, v_new, cache_k, cache_v, page_table, sb_bias, norm_attn):
    s, t_new, width = q.shape
    n_heads = width // HEAD_DIM
    n_pages = page_table.shape[1]
    pages = min(PAGES_PER_STEP, n_pages)
    steps = n_pages // pages
    n_phys = cache_k.shape[0]
    ckt = cache_k.transpose(0, 2, 3, 1).reshape(n_phys, width, PAGE_SIZE)
    cvt = cache_v.transpose(0, 2, 3, 1).reshape(n_phys, width, PAGE_SIZE)
    n_pad = V7X_SUBLANES
    pad = ((0, 0), (0, n_pad - t_new), (0, 0))
    knt = jnp.pad(k_new, pad).transpose(0, 2, 1)
    vn = jnp.pad(v_new, pad)
    bias_rows = jnp.repeat(sb_bias, t_new).reshape(n_heads * t_new, 1)
    rows = n_heads * t_new

    def page_spec(j):
        return pl.BlockSpec((1, width, PAGE_SIZE),
                            lambda b, st, pt: (pt[b, n_pages - 1 - (st * pages + j)], 0, 0))

    seq_spec = lambda n: pl.BlockSpec((1, n, width), lambda b, st, pt: (b, 0, 0))
    grid_spec = pltpu.PrefetchScalarGridSpec(
        num_scalar_prefetch=1,
        grid=(s, steps),
        in_specs=[seq_spec(t_new),
                  pl.BlockSpec((1, width, n_pad), lambda b, st, pt: (b, 0, 0)),
                  seq_spec(n_pad),
                  pl.BlockSpec((rows, 1), lambda b, st, pt: (0, 0)),
                  pl.BlockSpec((1, width), lambda b, st, pt: (0, 0))]
                 + [page_spec(j) for j in range(pages)] * 2,
        out_specs=seq_spec(t_new),
        scratch_shapes=[pltpu.VMEM((rows, width), _BF16), pltpu.VMEM((rows, width), _F32),
                        pltpu.VMEM((rows, 1), _F32)],
    )
    return pl.pallas_call(
        functools.partial(_attn_sample_kernel, n_heads=n_heads, t_new=t_new, pages=pages),
        out_shape=jax.ShapeDtypeStruct((s, t_new, width), _F32),
        grid_spec=grid_spec,
        compiler_params=_params("parallel", "arbitrary"),
        name="attn_sample",
    )(page_table, q, knt, vn, bias_rows, norm_attn.reshape(1, width),
      *([ckt] * pages), *([cvt] * pages))


def _scan_rows(a, b, h_prev):
    n = a.shape[0]
    row = lax.broadcasted_iota(jnp.int32, (n, 1), 0)
    if n <= V7X_SUBLANES:
        h = jnp.zeros_like(b)
        for t in range(n):
            h_prev = a[t:t + 1] * h_prev + b[t:t + 1]
            h = jnp.where(row == t, h_prev, h)
        return h
    d = 1
    while d < n:
        keep = row >= d
        a_prev = jnp.where(keep, pltpu.roll(a, d, 0), 1.0)
        b_prev = jnp.where(keep, pltpu.roll(b, d, 0), 0.0)
        b = b + a * b_prev
        a = a * a_prev
        d *= 2
    return b + a * h_prev


def _rglru_kernel(xr_ref, yb_ref, conv0_ref, h0_ref, cw_ref, cb_ref, wa_ref, ba_ref, wx_ref, bx_ref,
                  lam_ref, g_ref, rg_ref, conv_out_ref, h_out_ref, xbuf_ref, h_ref, *, tt):
    i = pl.program_id(1)
    hist = CONV_WIDTH - 1
    base = V7X_SUBLANES

    @pl.when(i == 0)
    def _():
        xbuf_ref[base - hist:base, :] = conv0_ref[0]
        h_ref[...] = h0_ref[0]

    xbuf_ref[base:base + tt, :] = xr_ref[0]
    xc = cb_ref[...]
    for j in range(CONV_WIDTH):
        xc = xc + cw_ref[j:j + 1, :] * xbuf_ref[base - hist + j:base - hist + j + tt, :]
    tail = xbuf_ref[base + tt - hist:base + tt, :]
    conv_out_ref[0] = tail
    xbuf_ref[base - hist:base, :] = tail

    r = jax.nn.sigmoid(_dot(xc, wa_ref[...]) + ba_ref[...])
    gate_i = jax.nn.sigmoid(_dot(xc, wx_ref[...]) + bx_ref[...])
    log_a = -RG_C * r * _softplus(-lam_ref[...])
    a = jnp.exp(log_a)
    b_in = jnp.sqrt(-jnp.tanh(log_a) * (a * a + 1.0)) * (gate_i * xc)
    h = _scan_rows(a, b_in, h_ref[...])
    h_last = h[tt - 1:tt, :]
    h_ref[...] = h_last
    h_out_ref[0] = h_last
    rg_ref[0] = _rms(h * jax.nn.gelu(yb_ref[0]), g_ref[...])


def _block_diag(w):
    n, r, c = w.shape
    eye = jnp.eye(n, dtype=w.dtype)
    return (eye[:, None, :, None] * w[:, :, None, :]).reshape(n * r, n * c)


def _rglru(xr, yb, conv0, h0, p, *, tt):
    b, t, c = xr.shape
    hist = CONV_WIDTH - 1
    wa = _block_diag(p["rg_wa"]).astype(_BF16)
    wx = _block_diag(p["rg_wx"]).astype(_BF16)
    row = lambda v: v.reshape(1, c)
    tile = pl.BlockSpec((1, tt, c), lambda bi, i: (bi, i, 0))
    const = lambda shape: pl.BlockSpec(shape, lambda bi, i: (0,) * len(shape))
    return pl.pallas_call(
        functools.partial(_rglru_kernel, tt=tt),
        out_shape=[jax.ShapeDtypeStruct((b, t, c), _F32),
                   jax.ShapeDtypeStruct((b, hist, c), _F32),
                   jax.ShapeDtypeStruct((b, 1, c), _F32)],
        grid=(b, t // tt),
        in_specs=[tile, tile,
                  pl.BlockSpec((1, hist, c), lambda bi, i: (bi, 0, 0)),
                  pl.BlockSpec((1, 1, c), lambda bi, i: (bi, 0, 0)),
                  const((CONV_WIDTH, c)), const((1, c)), const((c, c)), const((1, c)),
                  const((c, c)), const((1, c)), const((1, c)), const((1, c))],
        out_specs=[tile,
                   pl.BlockSpec((1, hist, c), lambda bi, i: (bi, 0, 0)),
                   pl.BlockSpec((1, 1, c), lambda bi, i: (bi, 0, 0))],
        scratch_shapes=[pltpu.VMEM((V7X_SUBLANES + tt, c), _F32), pltpu.VMEM((1, c), _F32)],
        compiler_params=_params("parallel", "arbitrary"),
        name="rglru",
    )(xr, yb, conv0, h0.reshape(b, 1, c), p["conv_w"], row(p["conv_b"]), wa, row(p["rg_ba"]),
      wx, row(p["rg_bx"]), row(p["rg_lambda"]), row(p["norm_rg"]))


ROUTE_EXPERT, ROUTE_GATE, ROUTE_RANK = 0, TOP_K, 2 * TOP_K


def _route(logits, counts):
    lane = lax.broadcasted_iota(jnp.int32, logits.shape, 1)
    lane_f = lane.astype(_F32)
    far = float(V7X_LANES)
    neg = -jnp.inf
    is_group = (lane >= N_EXPERTS) & (lane < N_EXPERTS + N_GROUPS)
    g_logit = jnp.where(is_group, logits, neg)
    g_max = jnp.max(g_logit, axis=-1, keepdims=True)
    grp = jnp.min(jnp.where(g_logit == g_max, lane_f - N_EXPERTS, far), axis=-1, keepdims=True)
    p_sel = 1.0 / jnp.sum(jnp.where(is_group, jnp.exp(logits - g_max), 0.0), axis=-1, keepdims=True)
    in_group = _div_pow2(lane, EXPERTS_PER_GROUP).astype(_F32) == grp
    e_logit = jnp.where(in_group, logits, neg)
    v1 = jnp.max(e_logit, axis=-1, keepdims=True)
    i1 = jnp.min(jnp.where(e_logit == v1, lane_f, far), axis=-1, keepdims=True)
    e_rest = jnp.where(lane_f == i1, neg, e_logit)
    v2 = jnp.max(e_rest, axis=-1, keepdims=True)
    i2 = jnp.min(jnp.where(e_rest == v2, lane_f, far), axis=-1, keepdims=True)
    t = jnp.exp(v2 - v1)
    g1 = p_sel / (1.0 + t)
    g2 = p_sel * t / (1.0 + t)
    n = logits.shape[0]
    chosen = jnp.where(lane_f == i1, 1.0, jnp.where(lane_f == i2, 1.0, 0.0))
    earlier = (lax.broadcasted_iota(jnp.int32, (n, n), 0) > lax.broadcasted_iota(jnp.int32, (n, n), 1))
    before = _dot(earlier.astype(_BF16), chosen) + counts
    r1 = jnp.sum(jnp.where(lane_f == i1, before, 0.0), axis=-1, keepdims=True)
    r2 = jnp.sum(jnp.where(lane_f == i2, before, 0.0), axis=-1, keepdims=True)
    route = jnp.zeros_like(logits)
    for offset, values in ((ROUTE_EXPERT, (i1, i2)), (ROUTE_GATE, (g1, g2)), (ROUTE_RANK, (r1, r2))):
        for k, val in enumerate(values):
            route = jnp.where(lane == offset + k, val, route)
    return route, counts + jnp.sum(chosen, axis=0, keepdims=True)


def _outproj_kernel(x_ref, attn_ref, rg_ref, gate_ref, scale_ref, shift_ref, wo_a_ref, wo_r_ref,
                    g_ref, wr_ref, br_ref, x2_ref, xn2_ref, route_ref, counts_ref):
    @pl.when((pl.program_id(0) == 0) & (pl.program_id(1) == 0))
    def _():
        counts_ref[...] = jnp.zeros_like(counts_ref)

    mix = _dot(attn_ref[0], wo_a_ref[...]) + _dot(rg_ref[0], wo_r_ref[...])
    x2 = x_ref[0] + gate_ref[0] * mix
    xn2 = _rms(x2, g_ref[...]) * (1.0 + scale_ref[0]) + shift_ref[0]
    x2_ref[0] = x2
    xn2_ref[0] = xn2
    route_ref[0], counts_ref[...] = _route(_dot(xn2, wr_ref[...]) + br_ref[...], counts_ref[...])


def _outproj(x, attn, rg, gate, scale, shift, p, *, tm):
    g, t, d = x.shape
    w_attn = attn.shape[-1]
    wdt = _BF16
    wo_a = p["w_out"][:w_attn].astype(wdt)
    wo_r = p["w_out"][w_attn:].astype(wdt)
    pad = V7X_LANES - N_EXPERTS - N_GROUPS
    w_route = jnp.pad(jnp.concatenate([p["w_router"], p["w_group"]], axis=1), ((0, 0), (0, pad))).astype(wdt)
    b_route = jnp.pad(jnp.concatenate([p["b_router"], p["b_group"]]), (0, pad)).reshape(1, V7X_LANES)
    mod_rows = gate.shape[1]
    mod_block = (1, 1, d) if mod_rows == 1 else (1, tm, d)
    mod_map = (lambda b, i: (b, 0, 0)) if mod_rows == 1 else (lambda b, i: (b, i, 0))
    mod_spec = pl.BlockSpec(mod_block, mod_map)
    tok = lambda w: pl.BlockSpec((1, tm, w), lambda b, i: (b, i, 0))
    const = lambda shape: pl.BlockSpec(shape, lambda b, i: (0,) * len(shape))
    return pl.pallas_call(
        _outproj_kernel,
        out_shape=[jax.ShapeDtypeStruct((g, t, d), _F32), jax.ShapeDtypeStruct((g, t, d), _F32),
                   jax.ShapeDtypeStruct((g, t, V7X_LANES), _F32), jax.ShapeDtypeStruct((1, V7X_LANES), _F32)],
        grid=(g, t // tm),
        in_specs=[tok(d), tok(w_attn), tok(rg.shape[-1]), mod_spec, mod_spec, mod_spec,
                  const(wo_a.shape), const(wo_r.shape), const((1, d)), const(w_route.shape),
                  const((1, V7X_LANES))],
        out_specs=[tok(d), tok(d), tok(V7X_LANES), const((1, V7X_LANES))],
        compiler_params=_params("arbitrary", "arbitrary"),
        name="outproj",
    )(x, attn, rg, gate, scale, shift, wo_a, wo_r, p["norm_ffn"].reshape(1, d), w_route, b_route)


def _moe_ffn_kernel(be_ref, nvalid_ref, x_ref, w1_ref, w3_ref, w2_ref, y_ref, w1b_ref, w3b_ref, w2b_ref):
    i = pl.program_id(0)
    new_expert = (i == 0) | (be_ref[i] != be_ref[jnp.maximum(i - 1, 0)])

    @pl.when(nvalid_ref[i] > 0)
    def _():
        @pl.when(new_expert)
        def _():
            w1b_ref[...] = w1_ref[0].astype(_BF16)
            w3b_ref[...] = w3_ref[0].astype(_BF16)
            w2b_ref[...] = w2_ref[0].astype(_BF16)

        xb = x_ref[...].astype(_BF16)
        up = _dot(xb, w1b_ref[...])
        hidden = up * jax.nn.sigmoid(up) * _dot(xb, w3b_ref[...])
        y_ref[...] = _dot(hidden, w2b_ref[...])

    @pl.when(nvalid_ref[i] == 0)
    def _():
        y_ref[...] = jnp.zeros_like(y_ref)


def _moe_layout(counts, n_blocks, block):
    padded = (counts + block - 1) // block * block
    pad_end = jnp.cumsum(padded)
    pad_start = pad_end - padded
    block_start = jnp.arange(n_blocks, dtype=jnp.int32) * block
    block_expert = jnp.sum((pad_end[None, :] <= block_start[:, None]).astype(jnp.int32), axis=1)
    block_expert = jnp.minimum(block_expert, N_EXPERTS - 1)
    n_valid = jnp.clip(pad_start[block_expert] + counts[block_expert] - block_start, 0, block).astype(jnp.int32)
    return pad_start, block_expert, n_valid


def _slot_rows(route, first_row):
    experts = route[:, ROUTE_EXPERT:ROUTE_EXPERT + TOP_K].astype(jnp.int32)
    rank = route[:, ROUTE_RANK:ROUTE_RANK + TOP_K].astype(jnp.int32)
    return first_row[experts] + rank


COPY_UNROLL = 16


def _for_each_copy(n, fn):
    assert n % COPY_UNROLL == 0
    lax.fori_loop(0, n, lambda j, _: (fn(j), 0)[1], 0, unroll=COPY_UNROLL)


def _dispatch_kernel(dest_ref, x_ref, rows_in_hbm, rows_hbm, sem_ref):
    del rows_in_hbm
    tm = x_ref.shape[1]

    def copy(j):
        return pltpu.make_async_copy(x_ref.at[0, pl.ds(_div_pow2(j, TOP_K), 1)],
                                     rows_hbm.at[pl.ds(dest_ref[0, 0, j], 1)], sem_ref.at[0])

    _for_each_copy(tm * TOP_K, lambda j: copy(j).start())
    _for_each_copy(tm * TOP_K, lambda j: copy(j).wait())


def _dispatch(xn2, dest, rows, *, tm):
    g, t, d = xn2.shape
    steps = t // tm
    dest_blocks = dest.reshape(g * steps, 1, tm * TOP_K)
    return pl.pallas_call(
        _dispatch_kernel,
        out_shape=jax.ShapeDtypeStruct(rows.shape, rows.dtype),
        grid=(g, steps),
        in_specs=[pl.BlockSpec((1, 1, tm * TOP_K), lambda b, i: (b * steps + i, 0, 0), memory_space=pltpu.SMEM),
                  pl.BlockSpec((1, tm, d), lambda b, i: (b, i, 0)),
                  pl.BlockSpec(memory_space=pl.ANY)],
        out_specs=pl.BlockSpec(memory_space=pl.ANY),
        scratch_shapes=[pltpu.SemaphoreType.DMA((1,))],
        input_output_aliases={2: 0},
        compiler_params=_params("arbitrary", "arbitrary"),
        name="moe_dispatch",
    )(dest_blocks, xn2, rows)


def _moe_ffn(rows, block_expert, n_valid, w1, w3, w2):
    n_rows, d = rows.shape
    d_exp = w1.shape[-1]
    block = MOE_BLOCK
    row_spec = pl.BlockSpec((block, d), lambda i, be, nv: (i, 0))
    grid_spec = pltpu.PrefetchScalarGridSpec(
        num_scalar_prefetch=2,
        grid=(n_rows // block,),
        in_specs=[row_spec,
                  pl.BlockSpec((1, d, d_exp), lambda i, be, nv: (be[i], 0, 0)),
                  pl.BlockSpec((1, d, d_exp), lambda i, be, nv: (be[i], 0, 0)),
                  pl.BlockSpec((1, d_exp, d), lambda i, be, nv: (be[i], 0, 0))],
        out_specs=row_spec,
        scratch_shapes=[pltpu.VMEM((d, d_exp), _BF16), pltpu.VMEM((d, d_exp), _BF16),
                        pltpu.VMEM((d_exp, d), _BF16)],
    )
    return pl.pallas_call(
        _moe_ffn_kernel,
        out_shape=jax.ShapeDtypeStruct((n_rows, d), _F32),
        grid_spec=grid_spec,
        compiler_params=_params("arbitrary"),
        name="moe_ffn",
    )(block_expert, n_valid, rows, w1, w3, w2)


def _final_kernel(dest_ref, x2_ref, route_ref, gate_ref, g_ref, y_hbm, o_ref, ybuf_ref, sem_ref):
    tm = x2_ref.shape[1]

    def copy(j):
        return pltpu.make_async_copy(y_hbm.at[pl.ds(dest_ref[0, 0, j], 1)],
                                     ybuf_ref.at[_mod_pow2(j, TOP_K), pl.ds(_div_pow2(j, TOP_K), 1)], sem_ref.at[0])

    _for_each_copy(tm * TOP_K, lambda j: copy(j).start())
    _for_each_copy(tm * TOP_K, lambda j: copy(j).wait())
    route = route_ref[0]
    ffn = route[:, ROUTE_GATE:ROUTE_GATE + 1] * ybuf_ref[0]
    for k in range(1, TOP_K):
        ffn = ffn + route[:, ROUTE_GATE + k:ROUTE_GATE + k + 1] * ybuf_ref[k]
    o_ref[0] = _rms(x2_ref[0] + gate_ref[0] * ffn, g_ref[...])


def _final(x2, route, dest, y_rows, gate, final_norm, *, tm):
    g, t, d = x2.shape
    mod_rows = gate.shape[1]
    mod_block = (1, 1, d) if mod_rows == 1 else (1, tm, d)
    mod_map = (lambda b, i: (b, 0, 0)) if mod_rows == 1 else (lambda b, i: (b, i, 0))
    steps = t // tm
    dest_blocks = dest.reshape(g * steps, 1, tm * TOP_K)
    tok = lambda w: pl.BlockSpec((1, tm, w), lambda b, i: (b, i, 0))
    return pl.pallas_call(
        _final_kernel,
        out_shape=jax.ShapeDtypeStruct((g, t, d), _F32),
        grid=(g, steps),
        in_specs=[pl.BlockSpec((1, 1, tm * TOP_K), lambda b, i: (b * steps + i, 0, 0), memory_space=pltpu.SMEM),
                  tok(d), tok(V7X_LANES),
                  pl.BlockSpec(mod_block, mod_map),
                  pl.BlockSpec((1, d), lambda b, i: (0, 0)),
                  pl.BlockSpec(memory_space=pl.ANY)],
        out_specs=tok(d),
        scratch_shapes=[pltpu.VMEM((TOP_K, tm, d), _F32), pltpu.SemaphoreType.DMA((1,))],
        compiler_params=_params("arbitrary", "arbitrary"),
        name="final",
    )(dest_blocks, x2, route, gate, final_norm.reshape(1, d), y_rows)


def _token_tile(t, want):
    return want if t % want == 0 else t


def kernel(x_prompt, x_sample, cache_k, cache_v, state_conv, state_h, page_table, c_prompt, c_sample,
           w_ada, b_ada, norm_mix, norm_ffn, w_in, sb_bias, norm_attn, conv_w, conv_b, rg_wa, rg_ba,
           rg_wx, rg_bx, rg_lambda, norm_rg, w_out, w_group, b_group, w_router, b_router, w1, w3, w2,
           final_norm):
    depth = w_ada.shape[0]
    assert depth == 1, "the final RMSNorm is fused into the layer's second residual add"
    bp, tp, d = x_prompt.shape
    bs, ts, _ = x_sample.shape
    n_s = bs * ts
    xp = x_prompt
    xs = x_sample.reshape(1, n_s, d)
    outs = [[] for _ in range(8)]
    for l in range(depth):
        p = dict(w_ada=w_ada[l], b_ada=b_ada[l], norm_mix=norm_mix[l], norm_ffn=norm_ffn[l], w_in=w_in[l],
                 sb_bias=sb_bias[l], norm_attn=norm_attn[l], conv_w=conv_w[l], conv_b=conv_b[l],
                 rg_wa=rg_wa[l], rg_ba=rg_ba[l], rg_wx=rg_wx[l], rg_bx=rg_bx[l], rg_lambda=rg_lambda[l],
                 norm_rg=norm_rg[l], w_out=w_out[l], w_group=w_group[l], b_group=b_group[l],
                 w_router=w_router[l], b_router=b_router[l])
        w_attn = p["w_in"].shape[1] // 5
        n_heads = w_attn // HEAD_DIM
        c_rg = w_attn
        mod = _ada(jnp.concatenate([c_prompt, c_sample], axis=0), p["w_ada"], p["b_ada"])
        mod_p = mod[:bp].reshape(bp, 6, 1, d)
        mod_s = jnp.repeat(mod[bp:].reshape(bs, 6, d), ts, axis=0).reshape(n_s, 6, d)
        shift1_p, scale1_p, gate1_p, shift2_p, scale2_p, gate2_p = (mod_p[:, j] for j in range(6))
        shift1_s, scale1_s, gate1_s, shift2_s, scale2_s, gate2_s = (mod_s[None, :, j] for j in range(6))

        w_in16 = p["w_in"].astype(_BF16)

        tm_p = _token_tile(tp, 512)
        k_p, v_p, xr_p, yb_p, q16, k16, v16 = _inproj(xp, scale1_p, shift1_p, p["norm_mix"], w_in16, tm=tm_p,
                                                      pair_major=True)
        attn_p = _attn_prompt(q16, k16, v16, p["sb_bias"], p["norm_attn"])
        conv0 = jnp.zeros((bp, CONV_WIDTH - 1, c_rg), _F32)
        h0 = jnp.zeros((bp, c_rg), _F32)
        rg_p, conv_p, h_p = _rglru(xr_p, yb_p, conv0, h0, p, tt=_token_tile(tp, 512))
        x2_p, xn2_p, route_p, counts_p = _outproj(xp, attn_p, rg_p, gate1_p, scale2_p, shift2_p, p, tm=tm_p)

        k_s, v_s, xr_s, yb_s, q_s = _inproj(xs, scale1_s, shift1_s, p["norm_mix"], w_in16, tm=n_s,
                                            pair_major=False)
        seq = lambda a: a.reshape(bs, ts, a.shape[-1])
        attn_s = _attn_sample(seq(q_s), seq(k_s), seq(v_s), cache_k[l], cache_v[l], page_table,
                              p["sb_bias"], p["norm_attn"])
        rg_s, conv_s, h_s = _rglru(seq(xr_s), seq(yb_s), state_conv[l], state_h[l], p, tt=ts)
        x2_s, xn2_s, route_s, counts_s = _outproj(xs, attn_s.reshape(1, n_s, w_attn), rg_s.reshape(1, n_s, c_rg),
                                                  gate1_s, scale2_s, shift2_s, p, tm=n_s)

        n_p = bp * tp
        n_blocks = pl.cdiv((n_p + n_s) * TOP_K + N_EXPERTS * (MOE_BLOCK - 1), MOE_BLOCK)
        slots_p = counts_p[0, :N_EXPERTS].astype(jnp.int32)
        slots_s = counts_s[0, :N_EXPERTS].astype(jnp.int32)
        first_row, block_expert, n_valid = _moe_layout(slots_p + slots_s, n_blocks, MOE_BLOCK)
        dest_p = _slot_rows(route_p.reshape(n_p, V7X_LANES), first_row)
        dest_s = _slot_rows(route_s.reshape(n_s, V7X_LANES), first_row + slots_p)
        rows = jnp.zeros((n_blocks * MOE_BLOCK, d), _F32)
        rows = _dispatch(xn2_p, dest_p, rows, tm=tm_p)
        rows = _dispatch(xn2_s, dest_s, rows, tm=n_s)
        y_rows = _moe_ffn(rows, block_expert, n_valid, w1[l], w3[l], w2[l])
        xp = _final(x2_p, route_p, dest_p, y_rows, gate2_p, final_norm, tm=tm_p)
        xs = _final(x2_s, route_s, dest_s, y_rows, gate2_s, final_norm, tm=n_s)
        hd = (n_heads, HEAD_DIM)
        for lst, val in zip(outs, (k_p.reshape(bp, tp, *hd), v_p.reshape(bp, tp, *hd), conv_p,
                                   h_p.reshape(bp, c_rg), seq(k_s).reshape(bs, ts, *hd),
                                   seq(v_s).reshape(bs, ts, *hd), conv_s, h_s.reshape(bs, c_rg))):
            lst.append(val)
    stacked = [jnp.stack(o) for o in outs]
    return (xp, xs.reshape(bs, ts, d), *stacked)
```

```python
import functools

import jax
import jax.numpy as jnp
from jax import lax
from jax.experimental import pallas as pl
from jax.experimental.pallas import tpu as pltpu

HEAD_DIM = 64
N_RG_BLOCKS = 8
CONV_WIDTH = 4
RG_C = 8.0
N_GROUPS = 4
EXPERTS_PER_GROUP = 8
N_EXPERTS = N_GROUPS * EXPERTS_PER_GROUP
TOP_K = 2
RMS_EPS = 1e-6
PAGE_SIZE = 128

V7X_LANES = 128
V7X_SUBLANES = 8
VMEM_LIMIT = 48 * 1024 * 1024
ATTN_VMEM_LIMIT = 56 * 1024 * 1024

MOE_BLOCK = 256
ATTN_TILE = 256
PAGES_PER_STEP = 16
MASKED_LOGIT = -1e30

_F32 = jnp.float32
_BF16 = jnp.bfloat16
_HIGHEST = lax.Precision.HIGHEST


def _dot(a, b):
    return jnp.dot(a.astype(_BF16), b.astype(_BF16), preferred_element_type=_F32)


def _dot_nt(a, b):
    return lax.dot_general(a.astype(_BF16), b.astype(_BF16), (((1,), (1,)), ((), ())),
                           preferred_element_type=_F32)


def _dot_f32(a, b):
    return jnp.dot(a, b, precision=_HIGHEST, preferred_element_type=_F32)


def _rms(x, gain):
    return x * lax.rsqrt(jnp.mean(x * x, axis=-1, keepdims=True) + RMS_EPS) * gain


def _softplus(z):
    return jnp.maximum(z, 0.0) + jnp.log1p(jnp.exp(-jnp.abs(z)))


def _softplus_logits(z):
    return jnp.maximum(z, 0.0) + jnp.log(1.0 + jnp.exp(-jnp.abs(z)))


def _div_pow2(x, n):
    assert n & (n - 1) == 0
    return lax.shift_right_logical(x, n.bit_length() - 1)


def _mod_pow2(x, n):
    assert n & (n - 1) == 0
    return lax.bitwise_and(x, n - 1)


def _params(*sem):
    return pltpu.CompilerParams(dimension_semantics=sem, vmem_limit_bytes=VMEM_LIMIT)


def _ada_kernel(c_ref, w_ref, b_ref, o_ref):
    c = c_ref[...]
    o_ref[...] = _dot(c * jax.nn.sigmoid(c), w_ref[...]) + b_ref[...]


def _ada(c, w_ada, b_ada):
    rows, d = c.shape
    n_chunks = w_ada.shape[1] // d
    return pl.pallas_call(
        _ada_kernel,
        out_shape=jax.ShapeDtypeStruct((rows, n_chunks * d), _F32),
        grid=(n_chunks,),
        in_specs=[pl.BlockSpec((rows, d), lambda j: (0, 0)),
                  pl.BlockSpec((d, d), lambda j: (0, j)),
                  pl.BlockSpec((1, d), lambda j: (0, j))],
        out_specs=pl.BlockSpec((rows, d), lambda j: (0, j)),
        compiler_params=_params("parallel"),
        name="ada",
    )(c, w_ada, b_ada.reshape(1, -1))


PAIR = 2 * HEAD_DIM


def _inproj_kernel(x_ref, scale_ref, shift_ref, g_ref, w_ref, k_ref, v_ref, xr_ref, yb_ref, q16_ref,
                   *kv16_refs, w_attn):
    xn = (_rms(x_ref[0], g_ref[...]) * (1.0 + scale_ref[0]) + shift_ref[0]).astype(_BF16)

    def col(j):
        return _dot(xn, w_ref[:, j * w_attn:(j + 1) * w_attn])

    q16 = (col(0) * (HEAD_DIM ** -0.5)).astype(_BF16)
    k = col(1)
    v = col(2)
    k_ref[0] = k
    v_ref[0] = v
    xr_ref[0] = col(3)
    yb_ref[0] = col(4)
    if kv16_refs:
        for ref, val in zip((q16_ref,) + kv16_refs, (q16, k.astype(_BF16), v.astype(_BF16))):
            for hp in range(w_attn // PAIR):
                ref[0, hp] = val[:, hp * PAIR:(hp + 1) * PAIR]
    else:
        q16_ref[0] = q16


def _inproj(x, scale, shift, gain, w_in, *, tm, pair_major):
    g, t, d = x.shape
    w_attn = w_in.shape[1] // 5
    mod_rows = scale.shape[1]
    mod_block = (1, 1, d) if mod_rows == 1 else (1, tm, d)
    mod_map = (lambda b, i: (b, 0, 0)) if mod_rows == 1 else (lambda b, i: (b, i, 0))
    out_block = pl.BlockSpec((1, tm, w_attn), lambda b, i: (b, i, 0))
    out_shape = [jax.ShapeDtypeStruct((g, t, w_attn), _F32)] * 4
    out_specs = [out_block] * 4
    if pair_major:
        n_pairs = w_attn // PAIR
        out_shape += [jax.ShapeDtypeStruct((g, n_pairs, t, PAIR), _BF16)] * 3
        out_specs += [pl.BlockSpec((1, n_pairs, tm, PAIR), lambda b, i: (b, 0, i, 0))] * 3
    else:
        out_shape += [jax.ShapeDtypeStruct((g, t, w_attn), _BF16)]
        out_specs += [out_block]
    return pl.pallas_call(
        functools.partial(_inproj_kernel, w_attn=w_attn),
        out_shape=out_shape,
        grid=(g, t // tm),
        in_specs=[pl.BlockSpec((1, tm, d), lambda b, i: (b, i, 0)),
                  pl.BlockSpec(mod_block, mod_map),
                  pl.BlockSpec(mod_block, mod_map),
                  pl.BlockSpec((1, d), lambda b, i: (0, 0)),
                  pl.BlockSpec(w_in.shape, lambda b, i: (0, 0))],
        out_specs=out_specs,
        compiler_params=_params("parallel", "parallel"),
        name="inproj",
    )(x, scale, shift, gain.reshape(1, d), w_in)


def _head_pair_norm(acc, gain, head0):
    sq = acc * acc
    s0 = jnp.sum(jnp.where(head0, sq, 0.0), axis=-1, keepdims=True)
    s1 = jnp.sum(jnp.where(head0, 0.0, sq), axis=-1, keepdims=True)
    ms = jnp.where(head0, s0, s1) * (1.0 / HEAD_DIM)
    return acc * lax.rsqrt(ms + RMS_EPS) * gain


def _attn_prompt_kernel(bias_ref, q_ref, k_ref, v_ref, g_ref, o_ref, off_ref, suffix_ref, q2_ref,
                        z_ref, lb_ref, tail_ref, rs_ref, w_ref, acc_ref, c_ref, *, tile, n_pairs, q_tiles):
    g = pl.program_id(1)
    lane = lax.broadcasted_iota(jnp.int32, (1, PAIR), 1)
    head0 = lane < HEAD_DIM

    @pl.when((pl.program_id(0) == 0) & (g == 0))
    def _():
        row = lax.broadcasted_iota(jnp.int32, (tile, tile), 0)
        col = lax.broadcasted_iota(jnp.int32, (tile, tile), 1)
        suffix_ref[...] = (row > col).astype(_BF16)
        row2 = lax.broadcasted_iota(jnp.int32, (2 * tile, tile), 0)
        col2 = lax.broadcasted_iota(jnp.int32, (2 * tile, tile), 1)
        causal = col2 < _mod_pow2(row2, tile)
        first = lax.broadcasted_iota(jnp.int32, (2 * tile, 1), 0) < tile
        for hp in range(n_pairs):
            bias = jnp.where(first, bias_ref[2 * hp], bias_ref[2 * hp + 1])
            off_ref[hp, 0] = jnp.broadcast_to(bias, (2 * tile, tile))
            off_ref[hp, 1] = jnp.where(causal, bias, MASKED_LOGIT)

    for qt in range(q_tiles):
        for hp in range(n_pairs):
            q = q_ref[0, hp, qt * tile:(qt + 1) * tile, :]
            zero = jnp.zeros_like(q)
            q2_ref[qt * n_pairs + hp] = jnp.concatenate([jnp.where(head0, q, zero), jnp.where(head0, zero, q)],
                                                        axis=0)
    acc_ref[...] = jnp.zeros_like(acc_ref)
    c_ref[...] = jnp.zeros_like(c_ref)

    assert n_pairs % 2 == 0
    per_stream = n_pairs // 2
    n_seg = q_tiles * per_stream

    def q_tile_of(seg):
        return q_tiles * g + _div_pow2(seg, per_stream)

    def advance(seg, i):
        i = i + 1
        wrap = (i == q_tile_of(seg) + 1).astype(jnp.int32)
        seg = seg + wrap
        i = i * (1 - wrap)
        done = (seg == n_seg).astype(jnp.int32)
        seg = seg - done
        return seg, i + done * q_tile_of(seg)

    def pair_of(x, seg):
        hp = x * per_stream + _mod_pow2(seg, per_stream)
        return hp, _div_pow2(seg, per_stream) * n_pairs + hp

    def rows_of(ref, hp, seg, i):
        return ref[0, hp, pl.ds(pl.multiple_of((q_tile_of(seg) - i) * tile, tile), tile), :]

    def scores(x, seg, i):
        hp, state = pair_of(x, seg)
        z_ref[x] = (_dot_nt(q2_ref[state], rows_of(k_ref, hp, seg, i))
                    + off_ref[hp, (i == 0).astype(jnp.int32)])

    def gates(x):
        z = z_ref[x]
        neg_z = -z
        log1m = jnp.minimum(neg_z, 0.0) - jnp.log(1.0 + jnp.exp(jnp.minimum(z, neg_z)))
        lb_ref[x] = z + log1m
        tail_ref[x] = _dot(log1m, suffix_ref[...])
        rs_ref[x] = jnp.sum(log1m, axis=-1, keepdims=True)

    def weights(x, seg):
        _, state = pair_of(x, seg)
        w_ref[x] = jnp.exp(lb_ref[x] + tail_ref[x] + c_ref[state]).astype(_BF16)
        c_ref[state] += rs_ref[x]

    def absorb(x, seg, i):
        hp, state = pair_of(x, seg)
        acc_ref[state] += _dot(w_ref[x], rows_of(v_ref, hp, seg, i))

    zero = jnp.int32(0)
    t0 = (zero, zero)
    t1 = advance(*t0)
    t2 = advance(*t1)
    for x in range(2):
        scores(x, *t0)
    for x in range(2):
        gates(x)
        scores(x, *t1)
    for x in range(2):
        weights(x, t0[0])
        gates(x)
        scores(x, *t2)

    def step(_, tiles):
        oldest, older, newest = tiles
        nxt = advance(*newest)
        for x in range(2):
            absorb(x, *oldest)
            weights(x, older[0])
            gates(x)
            scores(x, *nxt)
        return older, newest, nxt

    n_steps = per_stream * sum(q_tiles * g + qt + 1 for qt in range(q_tiles))
    lax.fori_loop(0, n_steps, step, (t0, t1, t2))
    for qt in range(q_tiles):
        for hp in range(n_pairs):
            acc = acc_ref[qt * n_pairs + hp]
            o_ref[0, qt * tile:(qt + 1) * tile, hp * PAIR:(hp + 1) * PAIR] = _head_pair_norm(
                jnp.where(head0, acc[:tile], acc[tile:]), g_ref[hp], head0)


def _attn_prompt(q16, k16, v16, sb_bias, norm_attn):
    b, n_pairs, t, _ = q16.shape
    tile = min(ATTN_TILE, t)
    n_q = t // tile
    q_tiles = next(c for c in (4, 2, 1) if n_q % c == 0)
    stage = lambda dtype: pltpu.VMEM((2, 2 * tile, tile), dtype)
    return pl.pallas_call(
        functools.partial(_attn_prompt_kernel, tile=tile, n_pairs=n_pairs, q_tiles=q_tiles),
        out_shape=jax.ShapeDtypeStruct((b, t, n_pairs * PAIR), _F32),
        grid=(b, t // (tile * q_tiles)),
        in_specs=[pl.BlockSpec(memory_space=pltpu.SMEM),
                  pl.BlockSpec((1, n_pairs, q_tiles * tile, PAIR), lambda bi, gi: (bi, 0, gi, 0)),
                  pl.BlockSpec((1, n_pairs, t, PAIR), lambda bi, gi: (bi, 0, 0, 0), pipeline_mode=pl.Buffered(1)),
                  pl.BlockSpec((1, n_pairs, t, PAIR), lambda bi, gi: (bi, 0, 0, 0), pipeline_mode=pl.Buffered(1)),
                  pl.BlockSpec((n_pairs, 1, PAIR), lambda bi, gi: (0, 0, 0))],
        out_specs=pl.BlockSpec((1, q_tiles * tile, n_pairs * PAIR), lambda bi, gi: (bi, gi, 0)),
        scratch_shapes=[pltpu.VMEM((n_pairs, 2, 2 * tile, tile), _F32),
                        pltpu.VMEM((tile, tile), _BF16),
                        pltpu.VMEM((q_tiles * n_pairs, 2 * tile, PAIR), _BF16),
                        stage(_F32),
                        stage(_F32),
                        stage(_F32),
                        pltpu.VMEM((2, 2 * tile, 1), _F32),
                        stage(_BF16),
                        pltpu.VMEM((q_tiles * n_pairs, 2 * tile, PAIR), _F32),
                        pltpu.VMEM((q_tiles * n_pairs, 2 * tile, 1), _F32)],
        compiler_params=pltpu.CompilerParams(dimension_semantics=("arbitrary", "arbitrary"),
                                             vmem_limit_bytes=ATTN_VMEM_LIMIT),
        name="attn_prompt",
    )(sb_bias, q16, k16, v16, norm_attn.reshape(n_pairs, 1, PAIR))


def _split3(x):
    hi = x.astype(_BF16)
    rest = x - hi.astype(_F32)
    mid = rest.astype(_BF16)
    return hi, mid, (rest - mid.astype(_F32)).astype(_BF16)


def _attn_sample_kernel(pt_ref, q_ref, knt_ref, vn_ref, bias_ref, g_ref, *rest, n_heads, t_new, pages):
    kt_refs = rest[:pages]
    vt_refs = rest[pages:2 * pages]
    o_ref, qbd_ref, acc_ref, c_ref = rest[2 * pages:]
    step = pl.program_id(1)
    rows = n_heads * t_new
    width = n_heads * HEAD_DIM
    r_id = lax.broadcasted_iota(jnp.int32, (rows, width), 0)
    l_id = lax.broadcasted_iota(jnp.int32, (rows, width), 1)
    own_head = _div_pow2(l_id, HEAD_DIM) == _div_pow2(r_id, t_new)
    bias = bias_ref[...]

    def suffix_matrix(n, dtype):
        return (lax.broadcasted_iota(jnp.int32, (n, n), 0) > lax.broadcasted_iota(jnp.int32, (n, n), 1)).astype(dtype)

    @pl.when(step == 0)
    def _():
        rep_r = lax.broadcasted_iota(jnp.int32, (rows, t_new), 0)
        rep_c = lax.broadcasted_iota(jnp.int32, (rows, t_new), 1)
        replicate = (_mod_pow2(rep_r, t_new) == rep_c).astype(_F32)
        q_rep = _dot(replicate, q_ref[0])
        qbd = jnp.where(own_head, q_rep, 0.0).astype(_BF16)
        qbd_ref[...] = qbd
        n_pad = knt_ref.shape[2]
        kr = _mod_pow2(lax.broadcasted_iota(jnp.int32, (rows, n_pad), 0), t_new)
        kc = lax.broadcasted_iota(jnp.int32, (rows, n_pad), 1)
        visible = kc < kr
        z = _dot(qbd, knt_ref[0]) + bias
        log1m = jnp.where(visible, -_softplus_logits(z), 0.0)
        tail = _dot_f32(log1m, suffix_matrix(n_pad, _F32))
        w = jnp.where(visible, jnp.exp(z + log1m + tail), 0.0)
        acc_ref[...] = _dot(w, vn_ref[0])
        c_ref[...] = jnp.sum(log1m, axis=-1, keepdims=True)

    qbd = qbd_ref[...]
    suffix = suffix_matrix(PAGE_SIZE, _BF16)
    zs = [_dot(qbd, kt_refs[j][0]) + bias for j in range(pages)]
    log1ms = [-_softplus_logits(z) for z in zs]
    tails = [sum(_dot(piece, suffix) for piece in _split3(l)) for l in log1ms]
    sums = [jnp.sum(l, axis=-1, keepdims=True) for l in log1ms]
    c = c_ref[...]
    acc = acc_ref[...]
    for j in range(pages):
        w = jnp.exp(zs[j] + log1ms[j] + tails[j] + c)
        acc = acc + _dot_nt(w, vt_refs[j][0])
        c = c + sums[j]
    acc_ref[...] = acc
    c_ref[...] = c

    @pl.when(step == pl.num_programs(1) - 1)
    def _():
        am = jnp.where(own_head, acc_ref[...], 0.0)
        ms = jnp.sum(am * am, axis=-1, keepdims=True) * (1.0 / HEAD_DIM)
        an = am * lax.rsqrt(ms + RMS_EPS)
        col_r = lax.broadcasted_iota(jnp.int32, (t_new, rows), 0)
        col_c = lax.broadcasted_iota(jnp.int32, (t_new, rows), 1)
        collect = (_mod_pow2(col_c, t_new) == col_r).astype(_F32)
        o_ref[0] = _dot_f32(collect, an) * g_ref[...]


def _attn_sample(q, k_new, v_new, cache_k, cache_v, page_table, sb_bias, norm_attn):
    s, t_new, width = q.shape
    n_heads = width // HEAD_DIM
    n_pages = page_table.shape[1]
    pages = min(PAGES_PER_STEP, n_pages)
    steps = n_pages // pages
    n_phys = cache_k.shape[0]
    ckt = cache_k.transpose(0, 2, 3, 1).reshape(n_phys, width, PAGE_SIZE)
    cvt = cache_v.transpose(0, 2, 3, 1).reshape(n_phys, width, PAGE_SIZE)
    n_pad = V7X_SUBLANES
    pad = ((0, 0), (0, n_pad - t_new), (0, 0))
    knt = jnp.pad(k_new, pad).transpose(0, 2, 1)
    vn = jnp.pad(v_new, pad)
    bias_rows = jnp.repeat(sb_bias, t_new).reshape(n_heads * t_new, 1)
    rows = n_heads * t_new

    def page_spec(j):
        return pl.BlockSpec((1, width, PAGE_SIZE),
                            lambda b, st, pt: (pt[b, n_pages - 1 - (st * pages + j)], 0, 0))

    seq_spec = lambda n: pl.BlockSpec((1, n, width), lambda b, st, pt: (b, 0, 0))
    grid_spec = pltpu.PrefetchScalarGridSpec(
        num_scalar_prefetch=1,
        grid=(s, steps),
        in_specs=[seq_spec(t_new),
                  pl.BlockSpec((1, width, n_pad), lambda b, st, pt: (b, 0, 0)),
                  seq_spec(n_pad),
                  pl.BlockSpec((rows, 1), lambda b, st, pt: (0, 0)),
                  pl.BlockSpec((1, width), lambda b, st, pt: (0, 0))]
                 + [page_spec(j) for j in range(pages)] * 2,
        out_specs=seq_spec(t_new),
        scratch_shapes=[pltpu.VMEM((rows, width), _BF16), pltpu.VMEM((rows, width), _F32),
                        pltpu.VMEM((rows, 1), _F32)],
    )
    return pl.pallas_call(
        functools.partial(_attn_sample_kernel, n_heads=n_heads, t_new=t_new, pages=pages),
        out_shape=jax.ShapeDtypeStruct((s, t_new, width), _F32),
        grid_spec=grid_spec,
        compiler_params=_params("parallel", "arbitrary"),
        name="attn_sample",
    )(page_table, q, knt, vn, bias_rows, norm_attn.reshape(1, width),
      *([ckt] * pages), *([cvt] * pages))


def _scan_rows(a, b, h_prev):
    n = a.shape[0]
    row = lax.broadcasted_iota(jnp.int32, (n, 1), 0)
    if n <= V7X_SUBLANES:
        h = jnp.zeros_like(b)
        for t in range(n):
            h_prev = a[t:t + 1] * h_prev + b[t:t + 1]
            h = jnp.where(row == t, h_prev, h)
        return h
    assert n % V7X_SUBLANES == 0
    sub = _mod_pow2(row, V7X_SUBLANES)
    d = 1
    while d < V7X_SUBLANES:
        keep = sub >= d
        a_prev = jnp.where(keep, pltpu.roll(a, d, 0), 1.0)
        b_prev = jnp.where(keep, pltpu.roll(b, d, 0), 0.0)
        b = b + a * b_prev
        a = a * a_prev
        d *= 2
    groups = []
    for g in range(n // V7X_SUBLANES):
        rows = slice(g * V7X_SUBLANES, (g + 1) * V7X_SUBLANES)
        h_group = b[rows] + a[rows] * h_prev
        groups.append(h_group)
        h_prev = h_group[V7X_SUBLANES - 1:V7X_SUBLANES]
    return jnp.concatenate(groups, axis=0)


def _rglru_kernel(xr_ref, yb_ref, conv0_ref, h0_ref, cw_ref, cb_ref, wa_ref, ba_ref, wx_ref, bx_ref,
                  lam_ref, g_ref, rg_ref, conv_out_ref, h_out_ref, xbuf_ref, h_ref, *, tt):
    i = pl.program_id(1)
    hist = CONV_WIDTH - 1
    base = V7X_SUBLANES

    @pl.when(i == 0)
    def _():
        xbuf_ref[base - hist:base, :] = conv0_ref[0]
        h_ref[...] = h0_ref[0]

    xbuf_ref[base:base + tt, :] = xr_ref[0]
    xc = cb_ref[...]
    for j in range(CONV_WIDTH):
        xc = xc + cw_ref[j:j + 1, :] * xbuf_ref[base - hist + j:base - hist + j + tt, :]
    tail = xbuf_ref[base + tt - hist:base + tt, :]
    conv_out_ref[0] = tail
    xbuf_ref[base - hist:base, :] = tail

    r = jax.nn.sigmoid(_dot(xc, wa_ref[...]) + ba_ref[...])
    gate_i = jax.nn.sigmoid(_dot(xc, wx_ref[...]) + bx_ref[...])
    log_a = -RG_C * r * _softplus(-lam_ref[...])
    a = jnp.exp(log_a)
    b_in = jnp.sqrt(-jnp.tanh(log_a) * (a * a + 1.0)) * (gate_i * xc)
    h = _scan_rows(a, b_in, h_ref[...])
    h_last = h[tt - 1:tt, :]
    h_ref[...] = h_last
    h_out_ref[0] = h_last
    rg_ref[0] = _rms(h * jax.nn.gelu(yb_ref[0]), g_ref[...])


def _block_diag(w):
    n, r, c = w.shape
    eye = jnp.eye(n, dtype=w.dtype)
    return (eye[:, None, :, None] * w[:, :, None, :]).reshape(n * r, n * c)


def _rglru(xr, yb, conv0, h0, p, *, tt):
    b, t, c = xr.shape
    hist = CONV_WIDTH - 1
    wa = _block_diag(p["rg_wa"]).astype(_BF16)
    wx = _block_diag(p["rg_wx"]).astype(_BF16)
    row = lambda v: v.reshape(1, c)
    tile = pl.BlockSpec((1, tt, c), lambda bi, i: (bi, i, 0))
    const = lambda shape: pl.BlockSpec(shape, lambda bi, i: (0,) * len(shape))
    return pl.pallas_call(
        functools.partial(_rglru_kernel, tt=tt),
        out_shape=[jax.ShapeDtypeStruct((b, t, c), _F32),
                   jax.ShapeDtypeStruct((b, hist, c), _F32),
                   jax.ShapeDtypeStruct((b, 1, c), _F32)],
        grid=(b, t // tt),
        in_specs=[tile, tile,
                  pl.BlockSpec((1, hist, c), lambda bi, i: (bi, 0, 0)),
                  pl.BlockSpec((1, 1, c), lambda bi, i: (bi, 0, 0)),
                  const((CONV_WIDTH, c)), const((1, c)), const((c, c)), const((1, c)),
                  const((c, c)), const((1, c)), const((1, c)), const((1, c))],
        out_specs=[tile,
                   pl.BlockSpec((1, hist, c), lambda bi, i: (bi, 0, 0)),
                   pl.BlockSpec((1, 1, c), lambda bi, i: (bi, 0, 0))],
        scratch_shapes=[pltpu.VMEM((V7X_SUBLANES + tt, c), _F32), pltpu.VMEM((1, c), _F32)],
        compiler_params=_params("parallel", "arbitrary"),
        name="rglru",
    )(xr, yb, conv0, h0.reshape(b, 1, c), p["conv_w"], row(p["conv_b"]), wa, row(p["rg_ba"]),
      wx, row(p["rg_bx"]), row(p["rg_lambda"]), row(p["norm_rg"]))


ROUTE_EXPERT, ROUTE_GATE, ROUTE_RANK = 0, TOP_K, 2 * TOP_K


def _route(logits, counts):
    lane = lax.broadcasted_iota(jnp.int32, logits.shape, 1)
    lane_f = lane.astype(_F32)
    far = float(V7X_LANES)
    neg = -jnp.inf
    is_group = (lane >= N_EXPERTS) & (lane < N_EXPERTS + N_GROUPS)
    g_logit = jnp.where(is_group, logits, neg)
    g_max = jnp.max(g_logit, axis=-1, keepdims=True)
    grp = jnp.min(jnp.where(g_logit == g_max, lane_f - N_EXPERTS, far), axis=-1, keepdims=True)
    p_sel = 1.0 / jnp.sum(jnp.where(is_group, jnp.exp(logits - g_max), 0.0), axis=-1, keepdims=True)
    in_group = _div_pow2(lane, EXPERTS_PER_GROUP).astype(_F32) == grp
    e_logit = jnp.where(in_group, logits, neg)
    v1 = jnp.max(e_logit, axis=-1, keepdims=True)
    i1 = jnp.min(jnp.where(e_logit == v1, lane_f, far), axis=-1, keepdims=True)
    e_rest = jnp.where(lane_f == i1, neg, e_logit)
    v2 = jnp.max(e_rest, axis=-1, keepdims=True)
    i2 = jnp.min(jnp.where(e_rest == v2, lane_f, far), axis=-1, keepdims=True)
    t = jnp.exp(v2 - v1)
    g1 = p_sel / (1.0 + t)
    g2 = p_sel * t / (1.0 + t)
    n = logits.shape[0]
    chosen = jnp.where(lane_f == i1, 1.0, jnp.where(lane_f == i2, 1.0, 0.0))
    earlier = (lax.broadcasted_iota(jnp.int32, (n, n), 0) > lax.broadcasted_iota(jnp.int32, (n, n), 1))
    before = _dot(earlier.astype(_BF16), chosen) + counts
    r1 = jnp.sum(jnp.where(lane_f == i1, before, 0.0), axis=-1, keepdims=True)
    r2 = jnp.sum(jnp.where(lane_f == i2, before, 0.0), axis=-1, keepdims=True)
    route = jnp.zeros_like(logits)
    for offset, values in ((ROUTE_EXPERT, (i1, i2)), (ROUTE_GATE, (g1, g2)), (ROUTE_RANK, (r1, r2))):
        for k, val in enumerate(values):
            route = jnp.where(lane == offset + k, val, route)
    return route, counts + jnp.sum(chosen, axis=0, keepdims=True)


def _outproj_kernel(x_ref, attn_ref, rg_ref, gate_ref, scale_ref, shift_ref, wo_a_ref, wo_r_ref,
                    g_ref, wr_ref, br_ref, x2_ref, xn2_ref, route_ref, counts_ref):
    @pl.when((pl.program_id(0) == 0) & (pl.program_id(1) == 0))
    def _():
        counts_ref[...] = jnp.zeros_like(counts_ref)

    mix = _dot(attn_ref[0], wo_a_ref[...]) + _dot(rg_ref[0], wo_r_ref[...])
    x2 = x_ref[0] + gate_ref[0] * mix
    xn2 = _rms(x2, g_ref[...]) * (1.0 + scale_ref[0]) + shift_ref[0]
    x2_ref[0] = x2
    xn2_ref[0] = xn2
    route_ref[0], counts_ref[...] = _route(_dot(xn2, wr_ref[...]) + br_ref[...], counts_ref[...])


def _outproj(x, attn, rg, gate, scale, shift, p, *, tm):
    g, t, d = x.shape
    w_attn = attn.shape[-1]
    wdt = _BF16
    wo_a = p["w_out"][:w_attn].astype(wdt)
    wo_r = p["w_out"][w_attn:].astype(wdt)
    pad = V7X_LANES - N_EXPERTS - N_GROUPS
    w_route = jnp.pad(jnp.concatenate([p["w_router"], p["w_group"]], axis=1), ((0, 0), (0, pad))).astype(wdt)
    b_route = jnp.pad(jnp.concatenate([p["b_router"], p["b_group"]]), (0, pad)).reshape(1, V7X_LANES)
    mod_rows = gate.shape[1]
    mod_block = (1, 1, d) if mod_rows == 1 else (1, tm, d)
    mod_map = (lambda b, i: (b, 0, 0)) if mod_rows == 1 else (lambda b, i: (b, i, 0))
    mod_spec = pl.BlockSpec(mod_block, mod_map)
    tok = lambda w: pl.BlockSpec((1, tm, w), lambda b, i: (b, i, 0))
    const = lambda shape: pl.BlockSpec(shape, lambda b, i: (0,) * len(shape))
    return pl.pallas_call(
        _outproj_kernel,
        out_shape=[jax.ShapeDtypeStruct((g, t, d), _F32), jax.ShapeDtypeStruct((g, t, d), _F32),
                   jax.ShapeDtypeStruct((g, t, V7X_LANES), _F32), jax.ShapeDtypeStruct((1, V7X_LANES), _F32)],
        grid=(g, t // tm),
        in_specs=[tok(d), tok(w_attn), tok(rg.shape[-1]), mod_spec, mod_spec, mod_spec,
                  const(wo_a.shape), const(wo_r.shape), const((1, d)), const(w_route.shape),
                  const((1, V7X_LANES))],
        out_specs=[tok(d), tok(d), tok(V7X_LANES), const((1, V7X_LANES))],
        compiler_params=_params("arbitrary", "arbitrary"),
        name="outproj",
    )(x, attn, rg, gate, scale, shift, wo_a, wo_r, p["norm_ffn"].reshape(1, d), w_route, b_route)


def _moe_ffn_kernel(be_ref, nvalid_ref, x_ref, w1_ref, w3_ref, w2_ref, y_ref, w1b_ref, w3b_ref, w2b_ref):
    i = pl.program_id(0)
    new_expert = (i == 0) | (be_ref[i] != be_ref[jnp.maximum(i - 1, 0)])

    @pl.when(nvalid_ref[i] > 0)
    def _():
        @pl.when(new_expert)
        def _():
            w1b_ref[...] = w1_ref[0].astype(_BF16)
            w3b_ref[...] = w3_ref[0].astype(_BF16)
            w2b_ref[...] = w2_ref[0].astype(_BF16)

        xb = x_ref[...].astype(_BF16)
        up = _dot(xb, w1b_ref[...])
        hidden = up * jax.nn.sigmoid(up) * _dot(xb, w3b_ref[...])
        y_ref[...] = _dot(hidden, w2b_ref[...])

    @pl.when(nvalid_ref[i] == 0)
    def _():
        y_ref[...] = jnp.zeros_like(y_ref)


def _moe_layout(counts, n_blocks, block):
    padded = (counts + block - 1) // block * block
    pad_end = jnp.cumsum(padded)
    pad_start = pad_end - padded
    block_start = jnp.arange(n_blocks, dtype=jnp.int32) * block
    block_expert = jnp.sum((pad_end[None, :] <= block_start[:, None]).astype(jnp.int32), axis=1)
    block_expert = jnp.minimum(block_expert, N_EXPERTS - 1)
    n_valid = jnp.clip(pad_start[block_expert] + counts[block_expert] - block_start, 0, block).astype(jnp.int32)
    return pad_start, block_expert, n_valid


def _slot_rows(route, first_row):
    experts = route[:, ROUTE_EXPERT:ROUTE_EXPERT + TOP_K].astype(jnp.int32)
    rank = route[:, ROUTE_RANK:ROUTE_RANK + TOP_K].astype(jnp.int32)
    return first_row[experts] + rank


COPY_UNROLL = 16


def _for_each_copy(n, fn):
    assert n % COPY_UNROLL == 0
    lax.fori_loop(0, n, lambda j, _: (fn(j), 0)[1], 0, unroll=COPY_UNROLL)


def _dispatch_kernel(dest_ref, x_ref, rows_in_hbm, rows_hbm, sem_ref):
    del rows_in_hbm
    tm = x_ref.shape[1]

    def copy(j):
        return pltpu.make_async_copy(x_ref.at[0, pl.ds(_div_pow2(j, TOP_K), 1)],
                                     rows_hbm.at[pl.ds(dest_ref[0, 0, j], 1)], sem_ref.at[0])

    _for_each_copy(tm * TOP_K, lambda j: copy(j).start())
    _for_each_copy(tm * TOP_K, lambda j: copy(j).wait())


def _dispatch(xn2, dest, rows, *, tm):
    g, t, d = xn2.shape
    steps = t // tm
    dest_blocks = dest.reshape(g * steps, 1, tm * TOP_K)
    return pl.pallas_call(
        _dispatch_kernel,
        out_shape=jax.ShapeDtypeStruct(rows.shape, rows.dtype),
        grid=(g, steps),
        in_specs=[pl.BlockSpec((1, 1, tm * TOP_K), lambda b, i: (b * steps + i, 0, 0), memory_space=pltpu.SMEM),
                  pl.BlockSpec((1, tm, d), lambda b, i: (b, i, 0)),
                  pl.BlockSpec(memory_space=pl.ANY)],
        out_specs=pl.BlockSpec(memory_space=pl.ANY),
        scratch_shapes=[pltpu.SemaphoreType.DMA((1,))],
        input_output_aliases={2: 0},
        compiler_params=_params("arbitrary", "arbitrary"),
        name="moe_dispatch",
    )(dest_blocks, xn2, rows)


def _moe_ffn(rows, block_expert, n_valid, w1, w3, w2):
    n_rows, d = rows.shape
    d_exp = w1.shape[-1]
    block = MOE_BLOCK
    row_spec = pl.BlockSpec((block, d), lambda i, be, nv: (i, 0))
    grid_spec = pltpu.PrefetchScalarGridSpec(
        num_scalar_prefetch=2,
        grid=(n_rows // block,),
        in_specs=[row_spec,
                  pl.BlockSpec((1, d, d_exp), lambda i, be, nv: (be[i], 0, 0)),
                  pl.BlockSpec((1, d, d_exp), lambda i, be, nv: (be[i], 0, 0)),
                  pl.BlockSpec((1, d_exp, d), lambda i, be, nv: (be[i], 0, 0))],
        out_specs=row_spec,
        scratch_shapes=[pltpu.VMEM((d, d_exp), _BF16), pltpu.VMEM((d, d_exp), _BF16),
                        pltpu.VMEM((d_exp, d), _BF16)],
    )
    return pl.pallas_call(
        _moe_ffn_kernel,
        out_shape=jax.ShapeDtypeStruct((n_rows, d), _F32),
        grid_spec=grid_spec,
        compiler_params=_params("arbitrary"),
        name="moe_ffn",
    )(block_expert, n_valid, rows, w1, w3, w2)


def _final_kernel(dest_ref, x2_ref, route_ref, gate_ref, g_ref, y_hbm, o_ref, ybuf_ref, sem_ref):
    tm = x2_ref.shape[1]

    def copy(j):
        return pltpu.make_async_copy(y_hbm.at[pl.ds(dest_ref[0, 0, j], 1)],
                                     ybuf_ref.at[_mod_pow2(j, TOP_K), pl.ds(_div_pow2(j, TOP_K), 1)], sem_ref.at[0])

    _for_each_copy(tm * TOP_K, lambda j: copy(j).start())
    _for_each_copy(tm * TOP_K, lambda j: copy(j).wait())
    route = route_ref[0]
    ffn = route[:, ROUTE_GATE:ROUTE_GATE + 1] * ybuf_ref[0]
    for k in range(1, TOP_K):
        ffn = ffn + route[:, ROUTE_GATE + k:ROUTE_GATE + k + 1] * ybuf_ref[k]
    o_ref[0] = _rms(x2_ref[0] + gate_ref[0] * ffn, g_ref[...])


def _final(x2, route, dest, y_rows, gate, final_norm, *, tm):
    g, t, d = x2.shape
    mod_rows = gate.shape[1]
    mod_block = (1, 1, d) if mod_rows == 1 else (1, tm, d)
    mod_map = (lambda b, i: (b, 0, 0)) if mod_rows == 1 else (lambda b, i: (b, i, 0))
    steps = t // tm
    dest_blocks = dest.reshape(g * steps, 1, tm * TOP_K)
    tok = lambda w: pl.BlockSpec((1, tm, w), lambda b, i: (b, i, 0))
    return pl.pallas_call(
        _final_kernel,
        out_shape=jax.ShapeDtypeStruct((g, t, d), _F32),
        grid=(g, steps),
        in_specs=[pl.BlockSpec((1, 1, tm * TOP_K), lambda b, i: (b * steps + i, 0, 0), memory_space=pltpu.SMEM),
                  tok(d), tok(V7X_LANES),
                  pl.BlockSpec(mod_block, mod_map),
                  pl.BlockSpec((1, d), lambda b, i: (0, 0)),
                  pl.BlockSpec(memory_space=pl.ANY)],
        out_specs=tok(d),
        scratch_shapes=[pltpu.VMEM((TOP_K, tm, d), _F32), pltpu.SemaphoreType.DMA((1,))],
        compiler_params=_params("arbitrary", "arbitrary"),
        name="final",
    )(dest_blocks, x2, route, gate, final_norm.reshape(1, d), y_rows)


def _token_tile(t, want):
    return want if t % want == 0 else t


def kernel(x_prompt, x_sample, cache_k, cache_v, state_conv, state_h, page_table, c_prompt, c_sample,
           w_ada, b_ada, norm_mix, norm_ffn, w_in, sb_bias, norm_attn, conv_w, conv_b, rg_wa, rg_ba,
           rg_wx, rg_bx, rg_lambda, norm_rg, w_out, w_group, b_group, w_router, b_router, w1, w3, w2,
           final_norm):
    depth = w_ada.shape[0]
    assert depth == 1, "the final RMSNorm is fused into the layer's second residual add"
    bp, tp, d = x_prompt.shape
    bs, ts, _ = x_sample.shape
    n_s = bs * ts
    xp = x_prompt
    xs = x_sample.reshape(1, n_s, d)
    outs = [[] for _ in range(8)]
    for l in range(depth):
        p = dict(w_ada=w_ada[l], b_ada=b_ada[l], norm_mix=norm_mix[l], norm_ffn=norm_ffn[l], w_in=w_in[l],
                 sb_bias=sb_bias[l], norm_attn=norm_attn[l], conv_w=conv_w[l], conv_b=conv_b[l],
                 rg_wa=rg_wa[l], rg_ba=rg_ba[l], rg_wx=rg_wx[l], rg_bx=rg_bx[l], rg_lambda=rg_lambda[l],
                 norm_rg=norm_rg[l], w_out=w_out[l], w_group=w_group[l], b_group=b_group[l],
                 w_router=w_router[l], b_router=b_router[l])
        w_attn = p["w_in"].shape[1] // 5
        n_heads = w_attn // HEAD_DIM
        c_rg = w_attn
        mod = _ada(jnp.concatenate([c_prompt, c_sample], axis=0), p["w_ada"], p["b_ada"])
        mod_p = mod[:bp].reshape(bp, 6, 1, d)
        mod_s = jnp.repeat(mod[bp:].reshape(bs, 6, d), ts, axis=0).reshape(n_s, 6, d)
        shift1_p, scale1_p, gate1_p, shift2_p, scale2_p, gate2_p = (mod_p[:, j] for j in range(6))
        shift1_s, scale1_s, gate1_s, shift2_s, scale2_s, gate2_s = (mod_s[None, :, j] for j in range(6))

        w_in16 = p["w_in"].astype(_BF16)

        tm_p = _token_tile(tp, 512)
        k_p, v_p, xr_p, yb_p, q16, k16, v16 = _inproj(xp, scale1_p, shift1_p, p["norm_mix"], w_in16, tm=tm_p,
                                                      pair_major=True)
        attn_p = _attn_prompt(q16, k16, v16, p["sb_bias"], p["norm_attn"])
        conv0 = jnp.zeros((bp, CONV_WIDTH - 1, c_rg), _F32)
        h0 = jnp.zeros((bp, c_rg), _F32)
        rg_p, conv_p, h_p = _rglru(xr_p, yb_p, conv0, h0, p, tt=_token_tile(tp, 512))
        x2_p, xn2_p, route_p, counts_p = _outproj(xp, attn_p, rg_p, gate1_p, scale2_p, shift2_p, p, tm=tm_p)

        k_s, v_s, xr_s, yb_s, q_s = _inproj(xs, scale1_s, shift1_s, p["norm_mix"], w_in16, tm=n_s,
                                            pair_major=False)
        seq = lambda a: a.reshape(bs, ts, a.shape[-1])
        attn_s = _attn_sample(seq(q_s), seq(k_s), seq(v_s), cache_k[l], cache_v[l], page_table,
                              p["sb_bias"], p["norm_attn"])
        rg_s, conv_s, h_s = _rglru(seq(xr_s), seq(yb_s), state_conv[l], state_h[l], p, tt=ts)
        x2_s, xn2_s, route_s, counts_s = _outproj(xs, attn_s.reshape(1, n_s, w_attn), rg_s.reshape(1, n_s, c_rg),
                                                  gate1_s, scale2_s, shift2_s, p, tm=n_s)

        n_p = bp * tp
        n_blocks = pl.cdiv((n_p + n_s) * TOP_K + N_EXPERTS * (MOE_BLOCK - 1), MOE_BLOCK)
        slots_p = counts_p[0, :N_EXPERTS].astype(jnp.int32)
        slots_s = counts_s[0, :N_EXPERTS].astype(jnp.int32)
        first_row, block_expert, n_valid = _moe_layout(slots_p + slots_s, n_blocks, MOE_BLOCK)
        dest_p = _slot_rows(route_p.reshape(n_p, V7X_LANES), first_row)
        dest_s = _slot_rows(route_s.reshape(n_s, V7X_LANES), first_row + slots_p)
        rows = jnp.zeros((n_blocks * MOE_BLOCK, d), _F32)
        rows = _dispatch(xn2_p, dest_p, rows, tm=tm_p)
        rows = _dispatch(xn2_s, dest_s, rows, tm=n_s)
        y_rows = _moe_ffn(rows, block_expert, n_valid, w1[l], w3[l], w2[l])
        xp = _final(x2_p, route_p, dest_p, y_rows, gate2_p, final_norm, tm=tm_p)
        xs = _final(x2_s, route_s, dest_s, y_rows, gate2_s, final_norm, tm=n_s)
        hd = (n_heads, HEAD_DIM)
        for lst, val in zip(outs, (k_p.reshape(bp, tp, *hd), v_p.reshape(bp, tp, *hd), conv_p,
                                   h_p.reshape(bp, c_rg), seq(k_s).reshape(bs, ts, *hd),
                                   seq(v_s).reshape(bs, ts, *hd), conv_s, h_s.reshape(bs, c_rg))):
            lst.append(val)
    stacked = [jnp.stack(o) for o in outs]
    return (xp, xs.reshape(bs, ts, d), *stacked)
```

```python
import functools

import jax
import jax.numpy as jnp
from jax import lax
from jax.experimental import pallas as pl
from jax.experimental.pallas import tpu as pltpu

HEAD_DIM = 64
N_RG_BLOCKS = 8
CONV_WIDTH = 4
RG_C = 8.0
N_GROUPS = 4
EXPERTS_PER_GROUP = 8
N_EXPERTS = N_GROUPS * EXPERTS_PER_GROUP
TOP_K = 2
RMS_EPS = 1e-6
PAGE_SIZE = 128

V7X_LANES = 128
V7X_SUBLANES = 8
VMEM_LIMIT = 48 * 1024 * 1024
ATTN_VMEM_LIMIT = 56 * 1024 * 1024

MOE_BLOCK = 512
ATTN_TILE = 256
PAGES_PER_STEP = 32
MASKED_LOGIT = -1e30

_F32 = jnp.float32
_BF16 = jnp.bfloat16
_HIGHEST = lax.Precision.HIGHEST


def _dot(a, b):
    return jnp.dot(a.astype(_BF16), b.astype(_BF16), preferred_element_type=_F32)


def _dot_nt(a, b):
    return lax.dot_general(a.astype(_BF16), b.astype(_BF16), (((1,), (1,)), ((), ())),
                           preferred_element_type=_F32)


def _dot_f32(a, b):
    return jnp.dot(a, b, precision=_HIGHEST, preferred_element_type=_F32)


def _rms(x, gain):
    return x * lax.rsqrt(jnp.mean(x * x, axis=-1, keepdims=True) + RMS_EPS) * gain


def _softplus(z):
    return jnp.maximum(z, 0.0) + jnp.log1p(jnp.exp(-jnp.abs(z)))


def _softplus_logits(z):
    return jnp.maximum(z, 0.0) + jnp.log(1.0 + jnp.exp(-jnp.abs(z)))


def _div_pow2(x, n):
    assert n & (n - 1) == 0
    return lax.shift_right_logical(x, n.bit_length() - 1)


def _mod_pow2(x, n):
    assert n & (n - 1) == 0
    return lax.bitwise_and(x, n - 1)


def _params(*sem):
    return pltpu.CompilerParams(dimension_semantics=sem, vmem_limit_bytes=VMEM_LIMIT)


def _ada_kernel(c_ref, w_ref, b_ref, o_ref):
    c = c_ref[...]
    o_ref[...] = _dot(c * jax.nn.sigmoid(c), w_ref[...]) + b_ref[...]


def _ada(c, w_ada, b_ada):
    rows, d = c.shape
    n_chunks = w_ada.shape[1] // d
    return pl.pallas_call(
        _ada_kernel,
        out_shape=jax.ShapeDtypeStruct((rows, n_chunks * d), _F32),
        grid=(n_chunks,),
        in_specs=[pl.BlockSpec((rows, d), lambda j: (0, 0)),
                  pl.BlockSpec((d, d), lambda j: (0, j)),
                  pl.BlockSpec((1, d), lambda j: (0, j))],
        out_specs=pl.BlockSpec((rows, d), lambda j: (0, j)),
        compiler_params=_params("parallel"),
        name="ada",
    )(c, w_ada, b_ada.reshape(1, -1))


PAIR = 2 * HEAD_DIM


def _inproj_kernel(x_ref, scale_ref, shift_ref, g_ref, w_ref, k_ref, v_ref, xr_ref, yb_ref, q16_ref,
                   *kv16_refs, w_attn):
    xn = (_rms(x_ref[0], g_ref[...]) * (1.0 + scale_ref[0]) + shift_ref[0]).astype(_BF16)

    def col(j):
        return _dot(xn, w_ref[:, j * w_attn:(j + 1) * w_attn])

    q16 = (col(0) * (HEAD_DIM ** -0.5)).astype(_BF16)
    k = col(1)
    v = col(2)
    k_ref[0] = k
    v_ref[0] = v
    xr_ref[0] = col(3)
    yb_ref[0] = col(4)
    if kv16_refs:
        for ref, val in zip((q16_ref,) + kv16_refs, (q16, k.astype(_BF16), v.astype(_BF16))):
            for hp in range(w_attn // PAIR):
                ref[0, hp] = val[:, hp * PAIR:(hp + 1) * PAIR]
    else:
        q16_ref[0] = q16


def _inproj(x, scale, shift, gain, w_in, *, tm, pair_major):
    g, t, d = x.shape
    w_attn = w_in.shape[1] // 5
    mod_rows = scale.shape[1]
    mod_block = (1, 1, d) if mod_rows == 1 else (1, tm, d)
    mod_map = (lambda b, i: (b, 0, 0)) if mod_rows == 1 else (lambda b, i: (b, i, 0))
    out_block = pl.BlockSpec((1, tm, w_attn), lambda b, i: (b, i, 0))
    out_shape = [jax.ShapeDtypeStruct((g, t, w_attn), _F32)] * 4
    out_specs = [out_block] * 4
    if pair_major:
        n_pairs = w_attn // PAIR
        out_shape += [jax.ShapeDtypeStruct((g, n_pairs, t, PAIR), _BF16)] * 3
        out_specs += [pl.BlockSpec((1, n_pairs, tm, PAIR), lambda b, i: (b, 0, i, 0))] * 3
    else:
        out_shape += [jax.ShapeDtypeStruct((g, t, w_attn), _BF16)]
        out_specs += [out_block]
    return pl.pallas_call(
        functools.partial(_inproj_kernel, w_attn=w_attn),
        out_shape=out_shape,
        grid=(g, t // tm),
        in_specs=[pl.BlockSpec((1, tm, d), lambda b, i: (b, i, 0)),
                  pl.BlockSpec(mod_block, mod_map),
                  pl.BlockSpec(mod_block, mod_map),
                  pl.BlockSpec((1, d), lambda b, i: (0, 0)),
                  pl.BlockSpec(w_in.shape, lambda b, i: (0, 0))],
        out_specs=out_specs,
        compiler_params=_params("parallel", "parallel"),
        name="inproj",
    )(x, scale, shift, gain.reshape(1, d), w_in)


def _head_pair_norm(acc, gain, head0):
    sq = acc * acc
    s0 = jnp.sum(jnp.where(head0, sq, 0.0), axis=-1, keepdims=True)
    s1 = jnp.sum(jnp.where(head0, 0.0, sq), axis=-1, keepdims=True)
    ms = jnp.where(head0, s0, s1) * (1.0 / HEAD_DIM)
    return acc * lax.rsqrt(ms + RMS_EPS) * gain


def _attn_prompt_kernel(bias_ref, q_ref, k_ref, v_ref, g_ref, o_ref, off_ref, suffix_ref, q2_ref,
                        z_ref, lb_ref, tail_ref, rs_ref, w_ref, acc_ref, c_ref, *, tile, n_pairs, q_tiles):
    g = pl.program_id(1)
    lane = lax.broadcasted_iota(jnp.int32, (1, PAIR), 1)
    head0 = lane < HEAD_DIM

    @pl.when((pl.program_id(0) == 0) & (g == 0))
    def _():
        row = lax.broadcasted_iota(jnp.int32, (tile, tile), 0)
        col = lax.broadcasted_iota(jnp.int32, (tile, tile), 1)
        suffix_ref[...] = (row > col).astype(_BF16)
        row2 = lax.broadcasted_iota(jnp.int32, (2 * tile, tile), 0)
        col2 = lax.broadcasted_iota(jnp.int32, (2 * tile, tile), 1)
        causal = col2 < _mod_pow2(row2, tile)
        first = lax.broadcasted_iota(jnp.int32, (2 * tile, 1), 0) < tile
        for hp in range(n_pairs):
            bias = jnp.where(first, bias_ref[2 * hp], bias_ref[2 * hp + 1])
            off_ref[hp, 0] = jnp.broadcast_to(bias, (2 * tile, tile))
            off_ref[hp, 1] = jnp.where(causal, bias, MASKED_LOGIT)

    for qt in range(q_tiles):
        for hp in range(n_pairs):
            q = q_ref[0, hp, qt * tile:(qt + 1) * tile, :]
            zero = jnp.zeros_like(q)
            q2_ref[qt * n_pairs + hp] = jnp.concatenate([jnp.where(head0, q, zero), jnp.where(head0, zero, q)],
                                                        axis=0)
    acc_ref[...] = jnp.zeros_like(acc_ref)
    c_ref[...] = jnp.zeros_like(c_ref)

    assert n_pairs % 2 == 0
    per_stream = n_pairs // 2
    n_seg = q_tiles * per_stream

    def q_tile_of(seg):
        return q_tiles * g + _div_pow2(seg, per_stream)

    def advance(seg, i):
        i = i + 1
        wrap = (i == q_tile_of(seg) + 1).astype(jnp.int32)
        seg = seg + wrap
        i = i * (1 - wrap)
        done = (seg == n_seg).astype(jnp.int32)
        seg = seg - done
        return seg, i + done * q_tile_of(seg)

    def pair_of(x, seg):
        hp = x * per_stream + _mod_pow2(seg, per_stream)
        return hp, _div_pow2(seg, per_stream) * n_pairs + hp

    def rows_of(ref, hp, seg, i):
        return ref[0, hp, pl.ds(pl.multiple_of((q_tile_of(seg) - i) * tile, tile), tile), :]

    def scores(x, seg, i):
        hp, state = pair_of(x, seg)
        z_ref[x] = (_dot_nt(q2_ref[state], rows_of(k_ref, hp, seg, i))
                    + off_ref[hp, (i == 0).astype(jnp.int32)])

    def gates(x):
        z = z_ref[x]
        neg_z = -z
        log1m = jnp.minimum(neg_z, 0.0) - jnp.log(1.0 + jnp.exp(jnp.minimum(z, neg_z)))
        lb_ref[x] = z + log1m
        tail_ref[x] = _dot(log1m, suffix_ref[...])
        rs_ref[x] = jnp.sum(log1m, axis=-1, keepdims=True)

    def weights(x, seg):
        _, state = pair_of(x, seg)
        w_ref[x] = jnp.exp(lb_ref[x] + tail_ref[x] + c_ref[state]).astype(_BF16)
        c_ref[state] += rs_ref[x]

    def absorb(x, seg, i):
        hp, state = pair_of(x, seg)
        acc_ref[state] += _dot(w_ref[x], rows_of(v_ref, hp, seg, i))

    zero = jnp.int32(0)
    t0 = (zero, zero)
    t1 = advance(*t0)
    t2 = advance(*t1)
    for x in range(2):
        scores(x, *t0)
    for x in range(2):
        gates(x)
        scores(x, *t1)
    for x in range(2):
        weights(x, t0[0])
        gates(x)
        scores(x, *t2)

    def step(_, tiles):
        oldest, older, newest = tiles
        nxt = advance(*newest)
        for x in range(2):
            absorb(x, *oldest)
            weights(x, older[0])
            gates(x)
            scores(x, *nxt)
        return older, newest, nxt

    n_steps = per_stream * sum(q_tiles * g + qt + 1 for qt in range(q_tiles))
    lax.fori_loop(0, n_steps, step, (t0, t1, t2))
    for qt in range(q_tiles):
        for hp in range(n_pairs):
            acc = acc_ref[qt * n_pairs + hp]
            o_ref[0, qt * tile:(qt + 1) * tile, hp * PAIR:(hp + 1) * PAIR] = _head_pair_norm(
                jnp.where(head0, acc[:tile], acc[tile:]), g_ref[hp], head0)


def _attn_prompt(q16, k16, v16, sb_bias, norm_attn):
    b, n_pairs, t, _ = q16.shape
    tile = min(ATTN_TILE, t)
    n_q = t // tile
    q_tiles = next(c for c in (4, 2, 1) if n_q % c == 0)
    stage = lambda dtype: pltpu.VMEM((2, 2 * tile, tile), dtype)
    return pl.pallas_call(
        functools.partial(_attn_prompt_kernel, tile=tile, n_pairs=n_pairs, q_tiles=q_tiles),
        out_shape=jax.ShapeDtypeStruct((b, t, n_pairs * PAIR), _F32),
        grid=(b, t // (tile * q_tiles)),
        in_specs=[pl.BlockSpec(memory_space=pltpu.SMEM),
                  pl.BlockSpec((1, n_pairs, q_tiles * tile, PAIR), lambda bi, gi: (bi, 0, gi, 0)),
                  pl.BlockSpec((1, n_pairs, t, PAIR), lambda bi, gi: (bi, 0, 0, 0), pipeline_mode=pl.Buffered(1)),
                  pl.BlockSpec((1, n_pairs, t, PAIR), lambda bi, gi: (bi, 0, 0, 0), pipeline_mode=pl.Buffered(1)),
                  pl.BlockSpec((n_pairs, 1, PAIR), lambda bi, gi: (0, 0, 0))],
        out_specs=pl.BlockSpec((1, q_tiles * tile, n_pairs * PAIR), lambda bi, gi: (bi, gi, 0)),
        scratch_shapes=[pltpu.VMEM((n_pairs, 2, 2 * tile, tile), _F32),
                        pltpu.VMEM((tile, tile), _BF16),
                        pltpu.VMEM((q_tiles * n_pairs, 2 * tile, PAIR), _BF16),
                        stage(_F32),
                        stage(_F32),
                        stage(_F32),
                        pltpu.VMEM((2, 2 * tile, 1), _F32),
                        stage(_BF16),
                        pltpu.VMEM((q_tiles * n_pairs, 2 * tile, PAIR), _F32),
                        pltpu.VMEM((q_tiles * n_pairs, 2 * tile, 1), _F32)],
        compiler_params=pltpu.CompilerParams(dimension_semantics=("arbitrary", "arbitrary"),
                                             vmem_limit_bytes=ATTN_VMEM_LIMIT),
        name="attn_prompt",
    )(sb_bias, q16, k16, v16, norm_attn.reshape(n_pairs, 1, PAIR))


def _split3(x):
    hi = x.astype(_BF16)
    rest = x - hi.astype(_F32)
    mid = rest.astype(_BF16)
    return hi, mid, (rest - mid.astype(_F32)).astype(_BF16)


def _attn_sample_kernel(pt_ref, q_ref, knt_ref, vn_ref, bias_ref, g_ref, *rest, n_heads, t_new, pages):
    kt_refs = rest[:pages]
    vt_refs = rest[pages:2 * pages]
    o_ref, qbd_ref, acc_ref, c_ref = rest[2 * pages:]
    step = pl.program_id(1)
    rows = n_heads * t_new
    width = n_heads * HEAD_DIM
    r_id = lax.broadcasted_iota(jnp.int32, (rows, width), 0)
    l_id = lax.broadcasted_iota(jnp.int32, (rows, width), 1)
    own_head = _div_pow2(l_id, HEAD_DIM) == _div_pow2(r_id, t_new)
    bias = bias_ref[...]

    def suffix_matrix(n, dtype):
        return (lax.broadcasted_iota(jnp.int32, (n, n), 0) > lax.broadcasted_iota(jnp.int32, (n, n), 1)).astype(dtype)

    @pl.when(step == 0)
    def _():
        rep_r = lax.broadcasted_iota(jnp.int32, (rows, t_new), 0)
        rep_c = lax.broadcasted_iota(jnp.int32, (rows, t_new), 1)
        replicate = (_mod_pow2(rep_r, t_new) == rep_c).astype(_F32)
        q_rep = _dot(replicate, q_ref[0])
        qbd = jnp.where(own_head, q_rep, 0.0).astype(_BF16)
        qbd_ref[...] = qbd
        n_pad = knt_ref.shape[2]
        kr = _mod_pow2(lax.broadcasted_iota(jnp.int32, (rows, n_pad), 0), t_new)
        kc = lax.broadcasted_iota(jnp.int32, (rows, n_pad), 1)
        visible = kc < kr
        z = _dot(qbd, knt_ref[0]) + bias
        log1m = jnp.where(visible, -_softplus_logits(z), 0.0)
        tail = _dot_f32(log1m, suffix_matrix(n_pad, _F32))
        w = jnp.where(visible, jnp.exp(z + log1m + tail), 0.0)
        acc_ref[...] = _dot(w, vn_ref[0])
        c_ref[...] = jnp.sum(log1m, axis=-1, keepdims=True)

    qbd = qbd_ref[...]
    suffix = suffix_matrix(PAGE_SIZE, _BF16)
    zs = [_dot(qbd, kt_refs[j][0]) + bias for j in range(pages)]
    log1ms = [-_softplus_logits(z) for z in zs]
    tails = [sum(_dot(piece, suffix) for piece in _split3(l)) for l in log1ms]
    sums = [jnp.sum(l, axis=-1, keepdims=True) for l in log1ms]
    c = c_ref[...]
    acc = acc_ref[...]
    for j in range(pages):
        w = jnp.exp(zs[j] + log1ms[j] + tails[j] + c)
        acc = acc + _dot_nt(w, vt_refs[j][0])
        c = c + sums[j]
    acc_ref[...] = acc
    c_ref[...] = c

    @pl.when(step == pl.num_programs(1) - 1)
    def _():
        am = jnp.where(own_head, acc_ref[...], 0.0)
        ms = jnp.sum(am * am, axis=-1, keepdims=True) * (1.0 / HEAD_DIM)
        an = am * lax.rsqrt(ms + RMS_EPS)
        col_r = lax.broadcasted_iota(jnp.int32, (t_new, rows), 0)
        col_c = lax.broadcasted_iota(jnp.int32, (t_new, rows), 1)
        collect = (_mod_pow2(col_c, t_new) == col_r).astype(_F32)
        o_ref[0] = _dot_f32(collect, an) * g_ref[...]


def _attn_sample(q, k_new, v_new, cache_k, cache_v, page_table, sb_bias, norm_attn):
    s, t_new, width = q.shape
    n_heads = width // HEAD_DIM
    n_pages = page_table.shape[1]
    pages = min(PAGES_PER_STEP, n_pages)
    steps = n_pages // pages
    n_phys = cache_k.shape[0]
    ckt = cache_k.transpose(0, 2, 3, 1).reshape(n_phys, width, PAGE_SIZE)
    cvt = cache_v.transpose(0, 2, 3, 1).reshape(n_phys, width, PAGE_SIZE)
    n_pad = V7X_SUBLANES
    pad = ((0, 0), (0, n_pad - t_new), (0, 0))
    knt = jnp.pad(k_new, pad).transpose(0, 2, 1)
    vn = jnp.pad(v_new, pad)
    bias_rows = jnp.repeat(sb_bias, t_new).reshape(n_heads * t_new, 1)
    rows = n_heads * t_new

    def page_spec(j):
        return pl.BlockSpec((1, width, PAGE_SIZE),
                            lambda b, st, pt: (pt[b, n_pages - 1 - (st * pages + j)], 0, 0))

    seq_spec = lambda n: pl.BlockSpec((1, n, width), lambda b, st, pt: (b, 0, 0))
    grid_spec = pltpu.PrefetchScalarGridSpec(
        num_scalar_prefetch=1,
        grid=(s, steps),
        in_specs=[seq_spec(t_new),
                  pl.BlockSpec((1, width, n_pad), lambda b, st, pt: (b, 0, 0)),
                  seq_spec(n_pad),
                  pl.BlockSpec((rows, 1), lambda b, st, pt: (0, 0)),
                  pl.BlockSpec((1, width), lambda b, st, pt: (0, 0))]
                 + [page_spec(j) for j in range(pages)] * 2,
        out_specs=seq_spec(t_new),
        scratch_shapes=[pltpu.VMEM((rows, width), _BF16), pltpu.VMEM((rows, width), _F32),
                        pltpu.VMEM((rows, 1), _F32)],
    )
    return pl.pallas_call(
        functools.partial(_attn_sample_kernel, n_heads=n_heads, t_new=t_new, pages=pages),
        out_shape=jax.ShapeDtypeStruct((s, t_new, width), _F32),
        grid_spec=grid_spec,
        compiler_params=_params("parallel", "arbitrary"),
        name="attn_sample",
    )(page_table, q, knt, vn, bias_rows, norm_attn.reshape(1, width),
      *([ckt] * pages), *([cvt] * pages))


def _scan_rows(a, b, h_prev):
    n = a.shape[0]
    row = lax.broadcasted_iota(jnp.int32, (n, 1), 0)
    if n <= V7X_SUBLANES:
        h = jnp.zeros_like(b)
        for t in range(n):
            h_prev = a[t:t + 1] * h_prev + b[t:t + 1]
            h = jnp.where(row == t, h_prev, h)
        return h
    assert n % V7X_SUBLANES == 0
    sub = _mod_pow2(row, V7X_SUBLANES)
    d = 1
    while d < V7X_SUBLANES:
        keep = sub >= d
        a_prev = jnp.where(keep, pltpu.roll(a, d, 0), 1.0)
        b_prev = jnp.where(keep, pltpu.roll(b, d, 0), 0.0)
        b = b + a * b_prev
        a = a * a_prev
        d *= 2
    groups = []
    for g in range(n // V7X_SUBLANES):
        rows = slice(g * V7X_SUBLANES, (g + 1) * V7X_SUBLANES)
        h_group = b[rows] + a[rows] * h_prev
        groups.append(h_group)
        h_prev = h_group[V7X_SUBLANES - 1:V7X_SUBLANES]
    return jnp.concatenate(groups, axis=0)


def _rglru_kernel(xr_ref, yb_ref, conv0_ref, h0_ref, cw_ref, cb_ref, wa_ref, ba_ref, wx_ref, bx_ref,
                  lam_ref, g_ref, rg_ref, conv_out_ref, h_out_ref, xbuf_ref, h_ref, *, tt):
    i = pl.program_id(1)
    hist = CONV_WIDTH - 1
    base = V7X_SUBLANES

    @pl.when(i == 0)
    def _():
        xbuf_ref[base - hist:base, :] = conv0_ref[0]
        h_ref[...] = h0_ref[0]

    xbuf_ref[base:base + tt, :] = xr_ref[0]
    xc = cb_ref[...]
    for j in range(CONV_WIDTH):
        xc = xc + cw_ref[j:j + 1, :] * xbuf_ref[base - hist + j:base - hist + j + tt, :]
    tail = xbuf_ref[base + tt - hist:base + tt, :]
    conv_out_ref[0] = tail
    xbuf_ref[base - hist:base, :] = tail

    r = jax.nn.sigmoid(_dot(xc, wa_ref[...]) + ba_ref[...])
    gate_i = jax.nn.sigmoid(_dot(xc, wx_ref[...]) + bx_ref[...])
    log_a = -RG_C * r * _softplus(-lam_ref[...])
    a = jnp.exp(log_a)
    b_in = jnp.sqrt(-jnp.tanh(log_a) * (a * a + 1.0)) * (gate_i * xc)
    h = _scan_rows(a, b_in, h_ref[...])
    h_last = h[tt - 1:tt, :]
    h_ref[...] = h_last
    h_out_ref[0] = h_last
    rg_ref[0] = _rms(h * jax.nn.gelu(yb_ref[0]), g_ref[...])


def _block_diag(w):
    n, r, c = w.shape
    eye = jnp.eye(n, dtype=w.dtype)
    return (eye[:, None, :, None] * w[:, :, None, :]).reshape(n * r, n * c)


def _rglru(xr, yb, conv0, h0, p, *, tt):
    b, t, c = xr.shape
    hist = CONV_WIDTH - 1
    wa = _block_diag(p["rg_wa"]).astype(_BF16)
    wx = _block_diag(p["rg_wx"]).astype(_BF16)
    row = lambda v: v.reshape(1, c)
    tile = pl.BlockSpec((1, tt, c), lambda bi, i: (bi, i, 0))
    const = lambda shape: pl.BlockSpec(shape, lambda bi, i: (0,) * len(shape))
    return pl.pallas_call(
        functools.partial(_rglru_kernel, tt=tt),
        out_shape=[jax.ShapeDtypeStruct((b, t, c), _F32),
                   jax.ShapeDtypeStruct((b, hist, c), _F32),
                   jax.ShapeDtypeStruct((b, 1, c), _F32)],
        grid=(b, t // tt),
        in_specs=[tile, tile,
                  pl.BlockSpec((1, hist, c), lambda bi, i: (bi, 0, 0)),
                  pl.BlockSpec((1, 1, c), lambda bi, i: (bi, 0, 0)),
                  const((CONV_WIDTH, c)), const((1, c)), const((c, c)), const((1, c)),
                  const((c, c)), const((1, c)), const((1, c)), const((1, c))],
        out_specs=[tile,
                   pl.BlockSpec((1, hist, c), lambda bi, i: (bi, 0, 0)),
                   pl.BlockSpec((1, 1, c), lambda bi, i: (bi, 0, 0))],
        scratch_shapes=[pltpu.VMEM((V7X_SUBLANES + tt, c), _F32), pltpu.VMEM((1, c), _F32)],
        compiler_params=_params("parallel", "arbitrary"),
        name="rglru",
    )(xr, yb, conv0, h0.reshape(b, 1, c), p["conv_w"], row(p["conv_b"]), wa, row(p["rg_ba"]),
      wx, row(p["rg_bx"]), row(p["rg_lambda"]), row(p["norm_rg"]))


ROUTE_EXPERT, ROUTE_GATE, ROUTE_RANK = 0, TOP_K, 2 * TOP_K


def _route(logits, counts):
    lane = lax.broadcasted_iota(jnp.int32, logits.shape, 1)
    lane_f = lane.astype(_F32)
    far = float(V7X_LANES)
    neg = -jnp.inf
    is_group = (lane >= N_EXPERTS) & (lane < N_EXPERTS + N_GROUPS)
    g_logit = jnp.where(is_group, logits, neg)
    g_max = jnp.max(g_logit, axis=-1, keepdims=True)
    grp = jnp.min(jnp.where(g_logit == g_max, lane_f - N_EXPERTS, far), axis=-1, keepdims=True)
    p_sel = 1.0 / jnp.sum(jnp.where(is_group, jnp.exp(logits - g_max), 0.0), axis=-1, keepdims=True)
    in_group = _div_pow2(lane, EXPERTS_PER_GROUP).astype(_F32) == grp
    e_logit = jnp.where(in_group, logits, neg)
    v1 = jnp.max(e_logit, axis=-1, keepdims=True)
    i1 = jnp.min(jnp.where(e_logit == v1, lane_f, far), axis=-1, keepdims=True)
    e_rest = jnp.where(lane_f == i1, neg, e_logit)
    v2 = jnp.max(e_rest, axis=-1, keepdims=True)
    i2 = jnp.min(jnp.where(e_rest == v2, lane_f, far), axis=-1, keepdims=True)
    t = jnp.exp(v2 - v1)
    g1 = p_sel / (1.0 + t)
    g2 = p_sel * t / (1.0 + t)
    n = logits.shape[0]
    chosen = jnp.where(lane_f == i1, 1.0, jnp.where(lane_f == i2, 1.0, 0.0))
    earlier = (lax.broadcasted_iota(jnp.int32, (n, n), 0) > lax.broadcasted_iota(jnp.int32, (n, n), 1))
    before = _dot(earlier.astype(_BF16), chosen) + counts
    r1 = jnp.sum(jnp.where(lane_f == i1, before, 0.0), axis=-1, keepdims=True)
    r2 = jnp.sum(jnp.where(lane_f == i2, before, 0.0), axis=-1, keepdims=True)
    route = jnp.zeros_like(logits)
    for offset, values in ((ROUTE_EXPERT, (i1, i2)), (ROUTE_GATE, (g1, g2)), (ROUTE_RANK, (r1, r2))):
        for k, val in enumerate(values):
            route = jnp.where(lane == offset + k, val, route)
    return route, counts + jnp.sum(chosen, axis=0, keepdims=True)


def _outproj_kernel(x_ref, attn_ref, rg_ref, gate_ref, scale_ref, shift_ref, wo_a_ref, wo_r_ref,
                    g_ref, wr_ref, br_ref, x2_ref, xn2_ref, route_ref, counts_ref):
    @pl.when((pl.program_id(0) == 0) & (pl.program_id(1) == 0))
    def _():
        counts_ref[...] = jnp.zeros_like(counts_ref)

    mix = _dot(attn_ref[0], wo_a_ref[...]) + _dot(rg_ref[0], wo_r_ref[...])
    x2 = x_ref[0] + gate_ref[0] * mix
    xn2 = _rms(x2, g_ref[...]) * (1.0 + scale_ref[0]) + shift_ref[0]
    x2_ref[0] = x2
    xn2_ref[0] = xn2
    route_ref[0], counts_ref[...] = _route(_dot(xn2, wr_ref[...]) + br_ref[...], counts_ref[...])


def _outproj(x, attn, rg, gate, scale, shift, p, *, tm):
    g, t, d = x.shape
    w_attn = attn.shape[-1]
    wdt = _BF16
    wo_a = p["w_out"][:w_attn].astype(wdt)
    wo_r = p["w_out"][w_attn:].astype(wdt)
    pad = V7X_LANES - N_EXPERTS - N_GROUPS
    w_route = jnp.pad(jnp.concatenate([p["w_router"], p["w_group"]], axis=1), ((0, 0), (0, pad))).astype(wdt)
    b_route = jnp.pad(jnp.concatenate([p["b_router"], p["b_group"]]), (0, pad)).reshape(1, V7X_LANES)
    mod_rows = gate.shape[1]
    mod_block = (1, 1, d) if mod_rows == 1 else (1, tm, d)
    mod_map = (lambda b, i: (b, 0, 0)) if mod_rows == 1 else (lambda b, i: (b, i, 0))
    mod_spec = pl.BlockSpec(mod_block, mod_map)
    tok = lambda w: pl.BlockSpec((1, tm, w), lambda b, i: (b, i, 0))
    const = lambda shape: pl.BlockSpec(shape, lambda b, i: (0,) * len(shape))
    return pl.pallas_call(
        _outproj_kernel,
        out_shape=[jax.ShapeDtypeStruct((g, t, d), _F32), jax.ShapeDtypeStruct((g, t, d), _F32),
                   jax.ShapeDtypeStruct((g, t, V7X_LANES), _F32), jax.ShapeDtypeStruct((1, V7X_LANES), _F32)],
        grid=(g, t // tm),
        in_specs=[tok(d), tok(w_attn), tok(rg.shape[-1]), mod_spec, mod_spec, mod_spec,
                  const(wo_a.shape), const(wo_r.shape), const((1, d)), const(w_route.shape),
                  const((1, V7X_LANES))],
        out_specs=[tok(d), tok(d), tok(V7X_LANES), const((1, V7X_LANES))],
        compiler_params=_params("arbitrary", "arbitrary"),
        name="outproj",
    )(x, attn, rg, gate, scale, shift, wo_a, wo_r, p["norm_ffn"].reshape(1, d), w_route, b_route)


def _moe_ffn_kernel(be_ref, nvalid_ref, x_ref, w1_ref, w3_ref, w2_ref, y_ref, w1b_ref, w3b_ref, w2b_ref):
    i = pl.program_id(0)
    new_expert = (i == 0) | (be_ref[i] != be_ref[jnp.maximum(i - 1, 0)])

    @pl.when(nvalid_ref[i] > 0)
    def _():
        @pl.when(new_expert)
        def _():
            w1b_ref[...] = w1_ref[0].astype(_BF16)
            w3b_ref[...] = w3_ref[0].astype(_BF16)
            w2b_ref[...] = w2_ref[0].astype(_BF16)

        xb = x_ref[...].astype(_BF16)
        up = _dot(xb, w1b_ref[...])
        hidden = up * jax.nn.sigmoid(up) * _dot(xb, w3b_ref[...])
        y_ref[...] = _dot(hidden, w2b_ref[...])

    @pl.when(nvalid_ref[i] == 0)
    def _():
        y_ref[...] = jnp.zeros_like(y_ref)


def _moe_layout(counts, n_blocks, block):
    padded = (counts + block - 1) // block * block
    pad_end = jnp.cumsum(padded)
    pad_start = pad_end - padded
    block_start = jnp.arange(n_blocks, dtype=jnp.int32) * block
    block_expert = jnp.sum((pad_end[None, :] <= block_start[:, None]).astype(jnp.int32), axis=1)
    block_expert = jnp.minimum(block_expert, N_EXPERTS - 1)
    n_valid = jnp.clip(pad_start[block_expert] + counts[block_expert] - block_start, 0, block).astype(jnp.int32)
    return pad_start, block_expert, n_valid


def _slot_rows(route, first_row):
    experts = route[:, ROUTE_EXPERT:ROUTE_EXPERT + TOP_K].astype(jnp.int32)
    rank = route[:, ROUTE_RANK:ROUTE_RANK + TOP_K].astype(jnp.int32)
    return first_row[experts] + rank


COPY_UNROLL = 16


def _for_each_copy(n, fn):
    assert n % COPY_UNROLL == 0
    lax.fori_loop(0, n, lambda j, _: (fn(j), 0)[1], 0, unroll=COPY_UNROLL)


def _dispatch_kernel(dest_ref, x_ref, rows_in_hbm, rows_hbm, sem_ref):
    del rows_in_hbm
    tm = x_ref.shape[1]

    def copy(j):
        return pltpu.make_async_copy(x_ref.at[0, pl.ds(_div_pow2(j, TOP_K), 1)],
                                     rows_hbm.at[pl.ds(dest_ref[0, 0, j], 1)], sem_ref.at[0])

    _for_each_copy(tm * TOP_K, lambda j: copy(j).start())
    _for_each_copy(tm * TOP_K, lambda j: copy(j).wait())


def _dispatch(xn2, dest, rows, *, tm):
    g, t, d = xn2.shape
    steps = t // tm
    dest_blocks = dest.reshape(g * steps, 1, tm * TOP_K)
    return pl.pallas_call(
        _dispatch_kernel,
        out_shape=jax.ShapeDtypeStruct(rows.shape, rows.dtype),
        grid=(g, steps),
        in_specs=[pl.BlockSpec((1, 1, tm * TOP_K), lambda b, i: (b * steps + i, 0, 0), memory_space=pltpu.SMEM),
                  pl.BlockSpec((1, tm, d), lambda b, i: (b, i, 0)),
                  pl.BlockSpec(memory_space=pl.ANY)],
        out_specs=pl.BlockSpec(memory_space=pl.ANY),
        scratch_shapes=[pltpu.SemaphoreType.DMA((1,))],
        input_output_aliases={2: 0},
        compiler_params=_params("arbitrary", "arbitrary"),
        name="moe_dispatch",
    )(dest_blocks, xn2, rows)


def _moe_ffn(rows, block_expert, n_valid, w1, w3, w2):
    n_rows, d = rows.shape
    d_exp = w1.shape[-1]
    block = MOE_BLOCK
    row_spec = pl.BlockSpec((block, d), lambda i, be, nv: (i, 0))
    grid_spec = pltpu.PrefetchScalarGridSpec(
        num_scalar_prefetch=2,
        grid=(n_rows // block,),
        in_specs=[row_spec,
                  pl.BlockSpec((1, d, d_exp), lambda i, be, nv: (be[i], 0, 0)),
                  pl.BlockSpec((1, d, d_exp), lambda i, be, nv: (be[i], 0, 0)),
                  pl.BlockSpec((1, d_exp, d), lambda i, be, nv: (be[i], 0, 0))],
        out_specs=row_spec,
        scratch_shapes=[pltpu.VMEM((d, d_exp), _BF16), pltpu.VMEM((d, d_exp), _BF16),
                        pltpu.VMEM((d_exp, d), _BF16)],
    )
    return pl.pallas_call(
        _moe_ffn_kernel,
        out_shape=jax.ShapeDtypeStruct((n_rows, d), _F32),
        grid_spec=grid_spec,
        compiler_params=_params("arbitrary"),
        name="moe_ffn",
    )(block_expert, n_valid, rows, w1, w3, w2)


def _final_kernel(dest_ref, x2_ref, route_ref, gate_ref, g_ref, y_hbm, o_ref, ybuf_ref, sem_ref):
    tm = x2_ref.shape[1]

    def copy(j):
        return pltpu.make_async_copy(y_hbm.at[pl.ds(dest_ref[0, 0, j], 1)],
                                     ybuf_ref.at[_mod_pow2(j, TOP_K), pl.ds(_div_pow2(j, TOP_K), 1)], sem_ref.at[0])

    _for_each_copy(tm * TOP_K, lambda j: copy(j).start())
    _for_each_copy(tm * TOP_K, lambda j: copy(j).wait())
    route = route_ref[0]
    ffn = route[:, ROUTE_GATE:ROUTE_GATE + 1] * ybuf_ref[0]
    for k in range(1, TOP_K):
        ffn = ffn + route[:, ROUTE_GATE + k:ROUTE_GATE + k + 1] * ybuf_ref[k]
    o_ref[0] = _rms(x2_ref[0] + gate_ref[0] * ffn, g_ref[...])


def _final(x2, route, dest, y_rows, gate, final_norm, *, tm):
    g, t, d = x2.shape
    mod_rows = gate.shape[1]
    mod_block = (1, 1, d) if mod_rows == 1 else (1, tm, d)
    mod_map = (lambda b, i: (b, 0, 0)) if mod_rows == 1 else (lambda b, i: (b, i, 0))
    steps = t // tm
    dest_blocks = dest.reshape(g * steps, 1, tm * TOP_K)
    tok = lambda w: pl.BlockSpec((1, tm, w), lambda b, i: (b, i, 0))
    return pl.pallas_call(
        _final_kernel,
        out_shape=jax.ShapeDtypeStruct((g, t, d), _F32),
        grid=(g, steps),
        in_specs=[pl.BlockSpec((1, 1, tm * TOP_K), lambda b, i: (b * steps + i, 0, 0), memory_space=pltpu.SMEM),
                  tok(d), tok(V7X_LANES),
                  pl.BlockSpec(mod_block, mod_map),
                  pl.BlockSpec((1, d), lambda b, i: (0, 0)),
                  pl.BlockSpec(memory_space=pl.ANY)],
        out_specs=tok(d),
        scratch_shapes=[pltpu.VMEM((TOP_K, tm, d), _F32), pltpu.SemaphoreType.DMA((1,))],
        compiler_params=_params("arbitrary", "arbitrary"),
        name="final",
    )(dest_blocks, x2, route, gate, final_norm.reshape(1, d), y_rows)


def _token_tile(t, want):
    return want if t % want == 0 else t


def kernel(x_prompt, x_sample, cache_k, cache_v, state_conv, state_h, page_table, c_prompt, c_sample,
           w_ada, b_ada, norm_mix, norm_ffn, w_in, sb_bias, norm_attn, conv_w, conv_b, rg_wa, rg_ba,
           rg_wx, rg_bx, rg_lambda, norm_rg, w_out, w_group, b_group, w_router, b_router, w1, w3, w2,
           final_norm):
    depth = w_ada.shape[0]
    assert depth == 1, "the final RMSNorm is fused into the layer's second residual add"
    bp, tp, d = x_prompt.shape
    bs, ts, _ = x_sample.shape
    n_s = bs * ts
    xp = x_prompt
    xs = x_sample.reshape(1, n_s, d)
    outs = [[] for _ in range(8)]
    for l in range(depth):
        p = dict(w_ada=w_ada[l], b_ada=b_ada[l], norm_mix=norm_mix[l], norm_ffn=norm_ffn[l], w_in=w_in[l],
                 sb_bias=sb_bias[l], norm_attn=norm_attn[l], conv_w=conv_w[l], conv_b=conv_b[l],
                 rg_wa=rg_wa[l], rg_ba=rg_ba[l], rg_wx=rg_wx[l], rg_bx=rg_bx[l], rg_lambda=rg_lambda[l],
                 norm_rg=norm_rg[l], w_out=w_out[l], w_group=w_group[l], b_group=b_group[l],
                 w_router=w_router[l], b_router=b_router[l])
        w_attn = p["w_in"].shape[1] // 5
        n_heads = w_attn // HEAD_DIM
        c_rg = w_attn
        mod = _ada(jnp.concatenate([c_prompt, c_sample], axis=0), p["w_ada"], p["b_ada"])
        mod_p = mod[:bp].reshape(bp, 6, 1, d)
        mod_s = jnp.repeat(mod[bp:].reshape(bs, 6, d), ts, axis=0).reshape(n_s, 6, d)
        shift1_p, scale1_p, gate1_p, shift2_p, scale2_p, gate2_p = (mod_p[:, j] for j in range(6))
        shift1_s, scale1_s, gate1_s, shift2_s, scale2_s, gate2_s = (mod_s[None, :, j] for j in range(6))

        w_in16 = p["w_in"].astype(_BF16)

        tm_p = _token_tile(tp, 512)
        k_p, v_p, xr_p, yb_p, q16, k16, v16 = _inproj(xp, scale1_p, shift1_p, p["norm_mix"], w_in16, tm=tm_p,
                                                      pair_major=True)
        attn_p = _attn_prompt(q16, k16, v16, p["sb_bias"], p["norm_attn"])
        conv0 = jnp.zeros((bp, CONV_WIDTH - 1, c_rg), _F32)
        h0 = jnp.zeros((bp, c_rg), _F32)
        rg_p, conv_p, h_p = _rglru(xr_p, yb_p, conv0, h0, p, tt=_token_tile(tp, 512))
        x2_p, xn2_p, route_p, counts_p = _outproj(xp, attn_p, rg_p, gate1_p, scale2_p, shift2_p, p, tm=tm_p)

        k_s, v_s, xr_s, yb_s, q_s = _inproj(xs, scale1_s, shift1_s, p["norm_mix"], w_in16, tm=n_s,
                                            pair_major=False)
        seq = lambda a: a.reshape(bs, ts, a.shape[-1])
        attn_s = _attn_sample(seq(q_s), seq(k_s), seq(v_s), cache_k[l], cache_v[l], page_table,
                              p["sb_bias"], p["norm_attn"])
        rg_s, conv_s, h_s = _rglru(seq(xr_s), seq(yb_s), state_conv[l], state_h[l], p, tt=ts)
        x2_s, xn2_s, route_s, counts_s = _outproj(xs, attn_s.reshape(1, n_s, w_attn), rg_s.reshape(1, n_s, c_rg),
                                                  gate1_s, scale2_s, shift2_s, p, tm=n_s)

        n_p = bp * tp
        n_blocks = pl.cdiv((n_p + n_s) * TOP_K + N_EXPERTS * (MOE_BLOCK - 1), MOE_BLOCK)
        slots_p = counts_p[0, :N_EXPERTS].astype(jnp.int32)
        slots_s = counts_s[0, :N_EXPERTS].astype(jnp.int32)
        first_row, block_expert, n_valid = _moe_layout(slots_p + slots_s, n_blocks, MOE_BLOCK)
        dest_p = _slot_rows(route_p.reshape(n_p, V7X_LANES), first_row)
        dest_s = _slot_rows(route_s.reshape(n_s, V7X_LANES), first_row + slots_p)
        rows = jnp.zeros((n_blocks * MOE_BLOCK, d), _F32)
        rows = _dispatch(xn2_p, dest_p, rows, tm=tm_p)
        rows = _dispatch(xn2_s, dest_s, rows, tm=n_s)
        y_rows = _moe_ffn(rows, block_expert, n_valid, w1[l], w3[l], w2[l])
        xp = _final(x2_p, route_p, dest_p, y_rows, gate2_p, final_norm, tm=tm_p)
        xs = _final(x2_s, route_s, dest_s, y_rows, gate2_s, final_norm, tm=n_s)
        hd = (n_heads, HEAD_DIM)
        for lst, val in zip(outs, (k_p.reshape(bp, tp, *hd), v_p.reshape(bp, tp, *hd), conv_p,
                                   h_p.reshape(bp, c_rg), seq(k_s).reshape(bs, ts, *hd),
                                   seq(v_s).reshape(bs, ts, *hd), conv_s, h_s.reshape(bs, c_rg))):
            lst.append(val)
    stacked = [jnp.stack(o) for o in outs]
    return (xp, xs.reshape(bs, ts, d), *stacked)
```

```python
import functools

import jax
import jax.numpy as jnp
from jax import lax
from jax.experimental import pallas as pl
from jax.experimental.pallas import tpu as pltpu

HEAD_DIM = 64
N_RG_BLOCKS = 8
CONV_WIDTH = 4
RG_C = 8.0
N_GROUPS = 4
EXPERTS_PER_GROUP = 8
N_EXPERTS = N_GROUPS * EXPERTS_PER_GROUP
TOP_K = 2
RMS_EPS = 1e-6
PAGE_SIZE = 128

V7X_LANES = 128
V7X_SUBLANES = 8
VMEM_LIMIT = 48 * 1024 * 1024
ATTN_VMEM_LIMIT = 56 * 1024 * 1024

MOE_BLOCK = 512
ATTN_TILE = 256
PAGES_PER_STEP = 32
MASKED_LOGIT = -1e30

_F32 = jnp.float32
_BF16 = jnp.bfloat16
_HIGHEST = lax.Precision.HIGHEST


def _dot(a, b):
    return jnp.dot(a.astype(_BF16), b.astype(_BF16), preferred_element_type=_F32)


def _dot_nt(a, b):
    return lax.dot_general(a.astype(_BF16), b.astype(_BF16), (((1,), (1,)), ((), ())),
                           preferred_element_type=_F32)


def _dot_f32(a, b):
    return jnp.dot(a, b, precision=_HIGHEST, preferred_element_type=_F32)


def _rms(x, gain):
    return x * lax.rsqrt(jnp.mean(x * x, axis=-1, keepdims=True) + RMS_EPS) * gain


def _softplus(z):
    return jnp.maximum(z, 0.0) + jnp.log1p(jnp.exp(-jnp.abs(z)))


def _softplus_logits(z):
    return jnp.maximum(z, 0.0) + jnp.log(1.0 + jnp.exp(-jnp.abs(z)))


def _div_pow2(x, n):
    assert n & (n - 1) == 0
    return lax.shift_right_logical(x, n.bit_length() - 1)


def _mod_pow2(x, n):
    assert n & (n - 1) == 0
    return lax.bitwise_and(x, n - 1)


def _params(*sem):
    return pltpu.CompilerParams(dimension_semantics=sem, vmem_limit_bytes=VMEM_LIMIT)


def _ada_kernel(c_ref, w_ref, b_ref, o_ref):
    c = c_ref[...]
    o_ref[...] = _dot(c * jax.nn.sigmoid(c), w_ref[...]) + b_ref[...]


def _ada(c, w_ada, b_ada):
    rows, d = c.shape
    n_chunks = w_ada.shape[1] // d
    return pl.pallas_call(
        _ada_kernel,
        out_shape=jax.ShapeDtypeStruct((rows, n_chunks * d), _F32),
        grid=(n_chunks,),
        in_specs=[pl.BlockSpec((rows, d), lambda j: (0, 0)),
                  pl.BlockSpec((d, d), lambda j: (0, j)),
                  pl.BlockSpec((1, d), lambda j: (0, j))],
        out_specs=pl.BlockSpec((rows, d), lambda j: (0, j)),
        compiler_params=_params("parallel"),
        name="ada",
    )(c, w_ada, b_ada.reshape(1, -1))


PAIR = 2 * HEAD_DIM


def _inproj_kernel(x_ref, scale_ref, shift_ref, g_ref, w_ref, k_ref, v_ref, xr_ref, yb_ref, q16_ref,
                   *kv16_refs, w_attn):
    xn = (_rms(x_ref[0], g_ref[...]) * (1.0 + scale_ref[0]) + shift_ref[0]).astype(_BF16)

    def col(j):
        return _dot(xn, w_ref[:, j * w_attn:(j + 1) * w_attn])

    q16 = (col(0) * (HEAD_DIM ** -0.5)).astype(_BF16)
    k = col(1)
    v = col(2)
    k_ref[0] = k
    v_ref[0] = v
    xr_ref[0] = col(3)
    yb_ref[0] = col(4)
    if kv16_refs:
        for ref, val in zip((q16_ref,) + kv16_refs, (q16, k.astype(_BF16), v.astype(_BF16))):
            for hp in range(w_attn // PAIR):
                ref[0, hp] = val[:, hp * PAIR:(hp + 1) * PAIR]
    else:
        q16_ref[0] = q16


def _inproj(x, scale, shift, gain, w_in, *, tm, pair_major):
    g, t, d = x.shape
    w_attn = w_in.shape[1] // 5
    mod_rows = scale.shape[1]
    mod_block = (1, 1, d) if mod_rows == 1 else (1, tm, d)
    mod_map = (lambda b, i: (b, 0, 0)) if mod_rows == 1 else (lambda b, i: (b, i, 0))
    out_block = pl.BlockSpec((1, tm, w_attn), lambda b, i: (b, i, 0))
    out_shape = [jax.ShapeDtypeStruct((g, t, w_attn), _F32)] * 4
    out_specs = [out_block] * 4
    if pair_major:
        n_pairs = w_attn // PAIR
        out_shape += [jax.ShapeDtypeStruct((g, n_pairs, t, PAIR), _BF16)] * 3
        out_specs += [pl.BlockSpec((1, n_pairs, tm, PAIR), lambda b, i: (b, 0, i, 0))] * 3
    else:
        out_shape += [jax.ShapeDtypeStruct((g, t, w_attn), _BF16)]
        out_specs += [out_block]
    return pl.pallas_call(
        functools.partial(_inproj_kernel, w_attn=w_attn),
        out_shape=out_shape,
        grid=(g, t // tm),
        in_specs=[pl.BlockSpec((1, tm, d), lambda b, i: (b, i, 0)),
                  pl.BlockSpec(mod_block, mod_map),
                  pl.BlockSpec(mod_block, mod_map),
                  pl.BlockSpec((1, d), lambda b, i: (0, 0)),
                  pl.BlockSpec(w_in.shape, lambda b, i: (0, 0))],
        out_specs=out_specs,
        compiler_params=_params("parallel", "parallel"),
        name="inproj",
    )(x, scale, shift, gain.reshape(1, d), w_in)


def _head_pair_norm(acc, gain, head0):
    sq = acc * acc
    s0 = jnp.sum(jnp.where(head0, sq, 0.0), axis=-1, keepdims=True)
    s1 = jnp.sum(jnp.where(head0, 0.0, sq), axis=-1, keepdims=True)
    ms = jnp.where(head0, s0, s1) * (1.0 / HEAD_DIM)
    return acc * lax.rsqrt(ms + RMS_EPS) * gain


def _attn_prompt_kernel(bias_ref, q_ref, k_ref, v_ref, g_ref, o_ref, off_ref, suffix_ref, q2_ref,
                        z_ref, lb_ref, tail_ref, rs_ref, w_ref, acc_ref, c_ref, *, tile, n_pairs, q_tiles):
    g = pl.program_id(1)
    lane = lax.broadcasted_iota(jnp.int32, (1, PAIR), 1)
    head0 = lane < HEAD_DIM

    @pl.when((pl.program_id(0) == 0) & (g == 0))
    def _():
        row = lax.broadcasted_iota(jnp.int32, (tile, tile), 0)
        col = lax.broadcasted_iota(jnp.int32, (tile, tile), 1)
        suffix_ref[...] = (row > col).astype(_BF16)
        row2 = lax.broadcasted_iota(jnp.int32, (2 * tile, tile), 0)
        col2 = lax.broadcasted_iota(jnp.int32, (2 * tile, tile), 1)
        causal = col2 < _mod_pow2(row2, tile)
        first = lax.broadcasted_iota(jnp.int32, (2 * tile, 1), 0) < tile
        for hp in range(n_pairs):
            bias = jnp.where(first, bias_ref[2 * hp], bias_ref[2 * hp + 1])
            off_ref[hp, 0] = jnp.broadcast_to(bias, (2 * tile, tile))
            off_ref[hp, 1] = jnp.where(causal, bias, MASKED_LOGIT)

    for qt in range(q_tiles):
        for hp in range(n_pairs):
            q = q_ref[0, hp, qt * tile:(qt + 1) * tile, :]
            zero = jnp.zeros_like(q)
            q2_ref[qt * n_pairs + hp] = jnp.concatenate([jnp.where(head0, q, zero), jnp.where(head0, zero, q)],
                                                        axis=0)
    acc_ref[...] = jnp.zeros_like(acc_ref)
    c_ref[...] = jnp.zeros_like(c_ref)

    assert n_pairs % 2 == 0
    per_stream = n_pairs // 2
    n_seg = q_tiles * per_stream

    def q_tile_of(seg):
        return q_tiles * g + _div_pow2(seg, per_stream)

    def advance(seg, i):
        i = i + 1
        wrap = (i == q_tile_of(seg) + 1).astype(jnp.int32)
        seg = seg + wrap
        i = i * (1 - wrap)
        done = (seg == n_seg).astype(jnp.int32)
        seg = seg - done
        return seg, i + done * q_tile_of(seg)

    def pair_of(x, seg):
        hp = x * per_stream + _mod_pow2(seg, per_stream)
        return hp, _div_pow2(seg, per_stream) * n_pairs + hp

    def rows_of(ref, hp, seg, i):
        return ref[0, hp, pl.ds(pl.multiple_of((q_tile_of(seg) - i) * tile, tile), tile), :]

    def scores(x, seg, i):
        hp, state = pair_of(x, seg)
        z_ref[x] = (_dot_nt(q2_ref[state], rows_of(k_ref, hp, seg, i))
                    + off_ref[hp, (i == 0).astype(jnp.int32)])

    def gates(x):
        z = z_ref[x]
        neg_z = -z
        log1m = jnp.minimum(neg_z, 0.0) - jnp.log(1.0 + jnp.exp(jnp.minimum(z, neg_z)))
        lb_ref[x] = z + log1m
        tail_ref[x] = _dot(log1m, suffix_ref[...])
        rs_ref[x] = jnp.sum(log1m, axis=-1, keepdims=True)

    def weights(x, seg):
        _, state = pair_of(x, seg)
        w_ref[x] = jnp.exp(lb_ref[x] + tail_ref[x] + c_ref[state]).astype(_BF16)
        c_ref[state] += rs_ref[x]

    def absorb(x, seg, i):
        hp, state = pair_of(x, seg)
        acc_ref[state] += _dot(w_ref[x], rows_of(v_ref, hp, seg, i))

    zero = jnp.int32(0)
    t0 = (zero, zero)
    t1 = advance(*t0)
    t2 = advance(*t1)
    for x in range(2):
        scores(x, *t0)
    for x in range(2):
        gates(x)
        scores(x, *t1)
    for x in range(2):
        weights(x, t0[0])
        gates(x)
        scores(x, *t2)

    def step(_, tiles):
        oldest, older, newest = tiles
        nxt = advance(*newest)
        for x in range(2):
            absorb(x, *oldest)
            weights(x, older[0])
            gates(x)
            scores(x, *nxt)
        return older, newest, nxt

    n_steps = per_stream * sum(q_tiles * g + qt + 1 for qt in range(q_tiles))
    lax.fori_loop(0, n_steps, step, (t0, t1, t2))
    for qt in range(q_tiles):
        for hp in range(n_pairs):
            acc = acc_ref[qt * n_pairs + hp]
            o_ref[0, qt * tile:(qt + 1) * tile, hp * PAIR:(hp + 1) * PAIR] = _head_pair_norm(
                jnp.where(head0, acc[:tile], acc[tile:]), g_ref[hp], head0)


def _attn_prompt(q16, k16, v16, sb_bias, norm_attn):
    b, n_pairs, t, _ = q16.shape
    tile = min(ATTN_TILE, t)
    n_q = t // tile
    q_tiles = next(c for c in (4, 2, 1) if n_q % c == 0)
    stage = lambda dtype: pltpu.VMEM((2, 2 * tile, tile), dtype)
    return pl.pallas_call(
        functools.partial(_attn_prompt_kernel, tile=tile, n_pairs=n_pairs, q_tiles=q_tiles),
        out_shape=jax.ShapeDtypeStruct((b, t, n_pairs * PAIR), _F32),
        grid=(b, t // (tile * q_tiles)),
        in_specs=[pl.BlockSpec(memory_space=pltpu.SMEM),
                  pl.BlockSpec((1, n_pairs, q_tiles * tile, PAIR), lambda bi, gi: (bi, 0, gi, 0)),
                  pl.BlockSpec((1, n_pairs, t, PAIR), lambda bi, gi: (bi, 0, 0, 0), pipeline_mode=pl.Buffered(1)),
                  pl.BlockSpec((1, n_pairs, t, PAIR), lambda bi, gi: (bi, 0, 0, 0), pipeline_mode=pl.Buffered(1)),
                  pl.BlockSpec((n_pairs, 1, PAIR), lambda bi, gi: (0, 0, 0))],
        out_specs=pl.BlockSpec((1, q_tiles * tile, n_pairs * PAIR), lambda bi, gi: (bi, gi, 0)),
        scratch_shapes=[pltpu.VMEM((n_pairs, 2, 2 * tile, tile), _F32),
                        pltpu.VMEM((tile, tile), _BF16),
                        pltpu.VMEM((q_tiles * n_pairs, 2 * tile, PAIR), _BF16),
                        stage(_F32),
                        stage(_F32),
                        stage(_F32),
                        pltpu.VMEM((2, 2 * tile, 1), _F32),
                        stage(_BF16),
                        pltpu.VMEM((q_tiles * n_pairs, 2 * tile, PAIR), _F32),
                        pltpu.VMEM((q_tiles * n_pairs, 2 * tile, 1), _F32)],
        compiler_params=pltpu.CompilerParams(dimension_semantics=("arbitrary", "arbitrary"),
                                             vmem_limit_bytes=ATTN_VMEM_LIMIT),
        name="attn_prompt",
    )(sb_bias, q16, k16, v16, norm_attn.reshape(n_pairs, 1, PAIR))


def _split3(x):
    hi = x.astype(_BF16)
    rest = x - hi.astype(_F32)
    mid = rest.astype(_BF16)
    return hi, mid, (rest - mid.astype(_F32)).astype(_BF16)


def _attn_sample_kernel(pt_ref, q_ref, knt_ref, vn_ref, bias_ref, g_ref, *rest, n_heads, t_new, pages):
    kt_refs = rest[:pages]
    vt_refs = rest[pages:2 * pages]
    o_ref, qbd_ref, acc_ref, c_ref = rest[2 * pages:]
    step = pl.program_id(1)
    rows = n_heads * t_new
    width = n_heads * HEAD_DIM
    r_id = lax.broadcasted_iota(jnp.int32, (rows, width), 0)
    l_id = lax.broadcasted_iota(jnp.int32, (rows, width), 1)
    own_head = _div_pow2(l_id, HEAD_DIM) == _div_pow2(r_id, t_new)
    bias = bias_ref[...]

    def suffix_matrix(n, dtype):
        return (lax.broadcasted_iota(jnp.int32, (n, n), 0) > lax.broadcasted_iota(jnp.int32, (n, n), 1)).astype(dtype)

    @pl.when(step == 0)
    def _():
        rep_r = lax.broadcasted_iota(jnp.int32, (rows, t_new), 0)
        rep_c = lax.broadcasted_iota(jnp.int32, (rows, t_new), 1)
        replicate = (_mod_pow2(rep_r, t_new) == rep_c).astype(_F32)
        q_rep = _dot(replicate, q_ref[0])
        qbd = jnp.where(own_head, q_rep, 0.0).astype(_BF16)
        qbd_ref[...] = qbd
        n_pad = knt_ref.shape[2]
        kr = _mod_pow2(lax.broadcasted_iota(jnp.int32, (rows, n_pad), 0), t_new)
        kc = lax.broadcasted_iota(jnp.int32, (rows, n_pad), 1)
        visible = kc < kr
        z = _dot(qbd, knt_ref[0]) + bias
        log1m = jnp.where(visible, -_softplus_logits(z), 0.0)
        tail = _dot_f32(log1m, suffix_matrix(n_pad, _F32))
        w = jnp.where(visible, jnp.exp(z + log1m + tail), 0.0)
        acc_ref[...] = _dot(w, vn_ref[0])
        c_ref[...] = jnp.sum(log1m, axis=-1, keepdims=True)

    qbd = qbd_ref[...]
    suffix = suffix_matrix(PAGE_SIZE, _BF16)
    zs = [_dot(qbd, kt_refs[j][0]) + bias for j in range(pages)]
    log1ms = [-_softplus_logits(z) for z in zs]
    tails = [sum(_dot(piece, suffix) for piece in _split3(l)) for l in log1ms]
    sums = [jnp.sum(l, axis=-1, keepdims=True) for l in log1ms]
    c = c_ref[...]
    acc = acc_ref[...]
    for j in range(pages):
        w = jnp.exp(zs[j] + log1ms[j] + tails[j] + c)
        acc = acc + _dot_nt(w, vt_refs[j][0])
        c = c + sums[j]
    acc_ref[...] = acc
    c_ref[...] = c

    @pl.when(step == pl.num_programs(1) - 1)
    def _():
        am = jnp.where(own_head, acc_ref[...], 0.0)
        ms = jnp.sum(am * am, axis=-1, keepdims=True) * (1.0 / HEAD_DIM)
        an = am * lax.rsqrt(ms + RMS_EPS)
        col_r = lax.broadcasted_iota(jnp.int32, (t_new, rows), 0)
        col_c = lax.broadcasted_iota(jnp.int32, (t_new, rows), 1)
        collect = (_mod_pow2(col_c, t_new) == col_r).astype(_F32)
        o_ref[0] = _dot_f32(collect, an) * g_ref[...]


def _attn_sample(q, k_new, v_new, cache_k, cache_v, page_table, sb_bias, norm_attn):
    s, t_new, width = q.shape
    n_heads = width // HEAD_DIM
    n_pages = page_table.shape[1]
    pages = min(PAGES_PER_STEP, n_pages)
    steps = n_pages // pages
    n_phys = cache_k.shape[0]
    ckt = cache_k.transpose(0, 2, 3, 1).reshape(n_phys, width, PAGE_SIZE)
    cvt = cache_v.transpose(0, 2, 3, 1).reshape(n_phys, width, PAGE_SIZE)
    n_pad = V7X_SUBLANES
    pad = ((0, 0), (0, n_pad - t_new), (0, 0))
    knt = jnp.pad(k_new, pad).transpose(0, 2, 1)
    vn = jnp.pad(v_new, pad)
    bias_rows = jnp.repeat(sb_bias, t_new).reshape(n_heads * t_new, 1)
    rows = n_heads * t_new

    def page_spec(j):
        return pl.BlockSpec((1, width, PAGE_SIZE),
                            lambda b, st, pt: (pt[b, n_pages - 1 - (st * pages + j)], 0, 0))

    seq_spec = lambda n: pl.BlockSpec((1, n, width), lambda b, st, pt: (b, 0, 0))
    grid_spec = pltpu.PrefetchScalarGridSpec(
        num_scalar_prefetch=1,
        grid=(s, steps),
        in_specs=[seq_spec(t_new),
                  pl.BlockSpec((1, width, n_pad), lambda b, st, pt: (b, 0, 0)),
                  seq_spec(n_pad),
                  pl.BlockSpec((rows, 1), lambda b, st, pt: (0, 0)),
                  pl.BlockSpec((1, width), lambda b, st, pt: (0, 0))]
                 + [page_spec(j) for j in range(pages)] * 2,
        out_specs=seq_spec(t_new),
        scratch_shapes=[pltpu.VMEM((rows, width), _BF16), pltpu.VMEM((rows, width), _F32),
                        pltpu.VMEM((rows, 1), _F32)],
    )
    return pl.pallas_call(
        functools.partial(_attn_sample_kernel, n_heads=n_heads, t_new=t_new, pages=pages),
        out_shape=jax.ShapeDtypeStruct((s, t_new, width), _F32),
        grid_spec=grid_spec,
        compiler_params=_params("parallel", "arbitrary"),
        name="attn_sample",
    )(page_table, q, knt, vn, bias_rows, norm_attn.reshape(1, width),
      *([ckt] * pages), *([cvt] * pages))


def _scan_rows(a, b, h_prev):
    n = a.shape[0]
    row = lax.broadcasted_iota(jnp.int32, (n, 1), 0)
    if n <= V7X_SUBLANES:
        h = jnp.zeros_like(b)
        for t in range(n):
            h_prev = a[t:t + 1] * h_prev + b[t:t + 1]
            h = jnp.where(row == t, h_prev, h)
        return h
    assert n % V7X_SUBLANES == 0
    sub = _mod_pow2(row, V7X_SUBLANES)
    d = 1
    while d < V7X_SUBLANES:
        keep = sub >= d
        a_prev = jnp.where(keep, pltpu.roll(a, d, 0), 1.0)
        b_prev = jnp.where(keep, pltpu.roll(b, d, 0), 0.0)
        b = b + a * b_prev
        a = a * a_prev
        d *= 2
    groups = []
    for g in range(n // V7X_SUBLANES):
        rows = slice(g * V7X_SUBLANES, (g + 1) * V7X_SUBLANES)
        h_group = b[rows] + a[rows] * h_prev
        groups.append(h_group)
        h_prev = h_group[V7X_SUBLANES - 1:V7X_SUBLANES]
    return jnp.concatenate(groups, axis=0)


def _rglru_kernel(xr_ref, yb_ref, conv0_ref, h0_ref, cw_ref, cb_ref, wa_ref, ba_ref, wx_ref, bx_ref,
                  lam_ref, g_ref, rg_ref, conv_out_ref, h_out_ref, xbuf_ref, h_ref, *, tt):
    i = pl.program_id(1)
    hist = CONV_WIDTH - 1
    base = V7X_SUBLANES

    @pl.when(i == 0)
    def _():
        xbuf_ref[base - hist:base, :] = conv0_ref[0]
        h_ref[...] = h0_ref[0]

    xbuf_ref[base:base + tt, :] = xr_ref[0]
    xc = cb_ref[...]
    for j in range(CONV_WIDTH):
        xc = xc + cw_ref[j:j + 1, :] * xbuf_ref[base - hist + j:base - hist + j + tt, :]
    tail = xbuf_ref[base + tt - hist:base + tt, :]
    conv_out_ref[0] = tail
    xbuf_ref[base - hist:base, :] = tail

    r = jax.nn.sigmoid(_dot(xc, wa_ref[...]) + ba_ref[...])
    gate_i = jax.nn.sigmoid(_dot(xc, wx_ref[...]) + bx_ref[...])
    log_a = -RG_C * r * _softplus(-lam_ref[...])
    a = jnp.exp(log_a)
    b_in = jnp.sqrt(-jnp.tanh(log_a) * (a * a + 1.0)) * (gate_i * xc)
    h = _scan_rows(a, b_in, h_ref[...])
    h_last = h[tt - 1:tt, :]
    h_ref[...] = h_last
    h_out_ref[0] = h_last
    rg_ref[0] = _rms(h * jax.nn.gelu(yb_ref[0]), g_ref[...])


def _block_diag(w):
    n, r, c = w.shape
    eye = jnp.eye(n, dtype=w.dtype)
    return (eye[:, None, :, None] * w[:, :, None, :]).reshape(n * r, n * c)


def _rglru(xr, yb, conv0, h0, p, *, tt):
    b, t, c = xr.shape
    hist = CONV_WIDTH - 1
    wa = _block_diag(p["rg_wa"]).astype(_BF16)
    wx = _block_diag(p["rg_wx"]).astype(_BF16)
    row = lambda v: v.reshape(1, c)
    tile = pl.BlockSpec((1, tt, c), lambda bi, i: (bi, i, 0))
    const = lambda shape: pl.BlockSpec(shape, lambda bi, i: (0,) * len(shape))
    return pl.pallas_call(
        functools.partial(_rglru_kernel, tt=tt),
        out_shape=[jax.ShapeDtypeStruct((b, t, c), _F32),
                   jax.ShapeDtypeStruct((b, hist, c), _F32),
                   jax.ShapeDtypeStruct((b, 1, c), _F32)],
        grid=(b, t // tt),
        in_specs=[tile, tile,
                  pl.BlockSpec((1, hist, c), lambda bi, i: (bi, 0, 0)),
                  pl.BlockSpec((1, 1, c), lambda bi, i: (bi, 0, 0)),
                  const((CONV_WIDTH, c)), const((1, c)), const((c, c)), const((1, c)),
                  const((c, c)), const((1, c)), const((1, c)), const((1, c))],
        out_specs=[tile,
                   pl.BlockSpec((1, hist, c), lambda bi, i: (bi, 0, 0)),
                   pl.BlockSpec((1, 1, c), lambda bi, i: (bi, 0, 0))],
        scratch_shapes=[pltpu.VMEM((V7X_SUBLANES + tt, c), _F32), pltpu.VMEM((1, c), _F32)],
        compiler_params=_params("parallel", "arbitrary"),
        name="rglru",
    )(xr, yb, conv0, h0.reshape(b, 1, c), p["conv_w"], row(p["conv_b"]), wa, row(p["rg_ba"]),
      wx, row(p["rg_bx"]), row(p["rg_lambda"]), row(p["norm_rg"]))


ROUTE_EXPERT, ROUTE_GATE, ROUTE_RANK = 0, TOP_K, 2 * TOP_K


def _route(logits, counts):
    lane = lax.broadcasted_iota(jnp.int32, logits.shape, 1)
    lane_f = lane.astype(_F32)
    far = float(V7X_LANES)
    neg = -jnp.inf
    is_group = (lane >= N_EXPERTS) & (lane < N_EXPERTS + N_GROUPS)
    g_logit = jnp.where(is_group, logits, neg)
    g_max = jnp.max(g_logit, axis=-1, keepdims=True)
    grp = jnp.min(jnp.where(g_logit == g_max, lane_f - N_EXPERTS, far), axis=-1, keepdims=True)
    p_sel = 1.0 / jnp.sum(jnp.where(is_group, jnp.exp(logits - g_max), 0.0), axis=-1, keepdims=True)
    in_group = _div_pow2(lane, EXPERTS_PER_GROUP).astype(_F32) == grp
    e_logit = jnp.where(in_group, logits, neg)
    v1 = jnp.max(e_logit, axis=-1, keepdims=True)
    i1 = jnp.min(jnp.where(e_logit == v1, lane_f, far), axis=-1, keepdims=True)
    e_rest = jnp.where(lane_f == i1, neg, e_logit)
    v2 = jnp.max(e_rest, axis=-1, keepdims=True)
    i2 = jnp.min(jnp.where(e_rest == v2, lane_f, far), axis=-1, keepdims=True)
    t = jnp.exp(v2 - v1)
    g1 = p_sel / (1.0 + t)
    g2 = p_sel * t / (1.0 + t)
    n = logits.shape[0]
    chosen = jnp.where(lane_f == i1, 1.0, jnp.where(lane_f == i2, 1.0, 0.0))
    earlier = (lax.broadcasted_iota(jnp.int32, (n, n), 0) > lax.broadcasted_iota(jnp.int32, (n, n), 1))
    before = _dot(earlier.astype(_BF16), chosen) + counts
    r1 = jnp.sum(jnp.where(lane_f == i1, before, 0.0), axis=-1, keepdims=True)
    r2 = jnp.sum(jnp.where(lane_f == i2, before, 0.0), axis=-1, keepdims=True)
    route = jnp.zeros_like(logits)
    for offset, values in ((ROUTE_EXPERT, (i1, i2)), (ROUTE_GATE, (g1, g2)), (ROUTE_RANK, (r1, r2))):
        for k, val in enumerate(values):
            route = jnp.where(lane == offset + k, val, route)
    return route, counts + jnp.sum(chosen, axis=0, keepdims=True)


def _outproj_kernel(x_ref, attn_ref, rg_ref, gate_ref, scale_ref, shift_ref, wo_a_ref, wo_r_ref,
                    g_ref, wr_ref, br_ref, x2_ref, xn2_ref, route_ref, counts_ref):
    @pl.when((pl.program_id(0) == 0) & (pl.program_id(1) == 0))
    def _():
        counts_ref[...] = jnp.zeros_like(counts_ref)

    mix = _dot(attn_ref[0], wo_a_ref[...]) + _dot(rg_ref[0], wo_r_ref[...])
    x2 = x_ref[0] + gate_ref[0] * mix
    xn2 = _rms(x2, g_ref[...]) * (1.0 + scale_ref[0]) + shift_ref[0]
    x2_ref[0] = x2
    xn2_ref[0] = xn2
    route_ref[0], counts_ref[...] = _route(_dot(xn2, wr_ref[...]) + br_ref[...], counts_ref[...])


def _outproj(x, attn, rg, gate, scale, shift, p, *, tm):
    g, t, d = x.shape
    w_attn = attn.shape[-1]
    wdt = _BF16
    wo_a = p["w_out"][:w_attn].astype(wdt)
    wo_r = p["w_out"][w_attn:].astype(wdt)
    pad = V7X_LANES - N_EXPERTS - N_GROUPS
    w_route = jnp.pad(jnp.concatenate([p["w_router"], p["w_group"]], axis=1), ((0, 0), (0, pad))).astype(wdt)
    b_route = jnp.pad(jnp.concatenate([p["b_router"], p["b_group"]]), (0, pad)).reshape(1, V7X_LANES)
    mod_rows = gate.shape[1]
    mod_block = (1, 1, d) if mod_rows == 1 else (1, tm, d)
    mod_map = (lambda b, i: (b, 0, 0)) if mod_rows == 1 else (lambda b, i: (b, i, 0))
    mod_spec = pl.BlockSpec(mod_block, mod_map)
    tok = lambda w: pl.BlockSpec((1, tm, w), lambda b, i: (b, i, 0))
    const = lambda shape: pl.BlockSpec(shape, lambda b, i: (0,) * len(shape))
    return pl.pallas_call(
        _outproj_kernel,
        out_shape=[jax.ShapeDtypeStruct((g, t, d), _F32), jax.ShapeDtypeStruct((g, t, d), _F32),
                   jax.ShapeDtypeStruct((g, t, V7X_LANES), _F32), jax.ShapeDtypeStruct((1, V7X_LANES), _F32)],
        grid=(g, t // tm),
        in_specs=[tok(d), tok(w_attn), tok(rg.shape[-1]), mod_spec, mod_spec, mod_spec,
                  const(wo_a.shape), const(wo_r.shape), const((1, d)), const(w_route.shape),
                  const((1, V7X_LANES))],
        out_specs=[tok(d), tok(d), tok(V7X_LANES), const((1, V7X_LANES))],
        compiler_params=_params("arbitrary", "arbitrary"),
        name="outproj",
    )(x, attn, rg, gate, scale, shift, wo_a, wo_r, p["norm_ffn"].reshape(1, d), w_route, b_route)


def _moe_ffn_kernel(be_ref, nvalid_ref, x_ref, w1_ref, w3_ref, w2_ref, y_ref, w1b_ref, w3b_ref, w2b_ref):
    i = pl.program_id(0)
    new_expert = (i == 0) | (be_ref[i] != be_ref[jnp.maximum(i - 1, 0)])

    @pl.when(nvalid_ref[i] > 0)
    def _():
        @pl.when(new_expert)
        def _():
            w1b_ref[...] = w1_ref[0].astype(_BF16)
            w3b_ref[...] = w3_ref[0].astype(_BF16)
            w2b_ref[...] = w2_ref[0].astype(_BF16)

        xb = x_ref[...].astype(_BF16)
        up = _dot(xb, w1b_ref[...])
        hidden = up * jax.nn.sigmoid(up) * _dot(xb, w3b_ref[...])
        y_ref[...] = _dot(hidden, w2b_ref[...])

    @pl.when(nvalid_ref[i] == 0)
    def _():
        y_ref[...] = jnp.zeros_like(y_ref)


def _moe_layout(counts, n_blocks, block):
    padded = (counts + block - 1) // block * block
    pad_end = jnp.cumsum(padded)
    pad_start = pad_end - padded
    block_start = jnp.arange(n_blocks, dtype=jnp.int32) * block
    block_expert = jnp.sum((pad_end[None, :] <= block_start[:, None]).astype(jnp.int32), axis=1)
    block_expert = jnp.minimum(block_expert, N_EXPERTS - 1)
    n_valid = jnp.clip(pad_start[block_expert] + counts[block_expert] - block_start, 0, block).astype(jnp.int32)
    return pad_start, block_expert, n_valid


def _slot_rows(route, first_row):
    experts = route[:, ROUTE_EXPERT:ROUTE_EXPERT + TOP_K].astype(jnp.int32)
    rank = route[:, ROUTE_RANK:ROUTE_RANK + TOP_K].astype(jnp.int32)
    return first_row[experts] + rank


TOKEN_UNROLL = 8


def _for_each_slot(n_tokens, fn):
    assert n_tokens % TOKEN_UNROLL == 0

    def body(r, _):
        for k in range(TOP_K):
            fn(r, k)
        return 0

    lax.fori_loop(0, n_tokens, body, 0, unroll=TOKEN_UNROLL)


def _dispatch_kernel(dest_ref, x_ref, rows_in_hbm, rows_hbm, sem_ref):
    del rows_in_hbm
    tm = x_ref.shape[1]

    def copy(r, k):
        return pltpu.make_async_copy(x_ref.at[0, pl.ds(r, 1)],
                                     rows_hbm.at[pl.ds(dest_ref[0, 0, r * TOP_K + k], 1)], sem_ref.at[0])

    _for_each_slot(tm, lambda r, k: copy(r, k).start())
    _for_each_slot(tm, lambda r, k: copy(r, k).wait())


def _dispatch(xn2, dest, rows, *, tm):
    g, t, d = xn2.shape
    steps = t // tm
    dest_blocks = dest.reshape(g * steps, 1, tm * TOP_K)
    return pl.pallas_call(
        _dispatch_kernel,
        out_shape=jax.ShapeDtypeStruct(rows.shape, rows.dtype),
        grid=(g, steps),
        in_specs=[pl.BlockSpec((1, 1, tm * TOP_K), lambda b, i: (b * steps + i, 0, 0), memory_space=pltpu.SMEM),
                  pl.BlockSpec((1, tm, d), lambda b, i: (b, i, 0)),
                  pl.BlockSpec(memory_space=pl.ANY)],
        out_specs=pl.BlockSpec(memory_space=pl.ANY),
        scratch_shapes=[pltpu.SemaphoreType.DMA((1,))],
        input_output_aliases={2: 0},
        compiler_params=_params("arbitrary", "arbitrary"),
        name="moe_dispatch",
    )(dest_blocks, xn2, rows)


def _moe_ffn(rows, block_expert, n_valid, w1, w3, w2):
    n_rows, d = rows.shape
    d_exp = w1.shape[-1]
    block = MOE_BLOCK
    row_spec = pl.BlockSpec((block, d), lambda i, be, nv: (i, 0))
    grid_spec = pltpu.PrefetchScalarGridSpec(
        num_scalar_prefetch=2,
        grid=(n_rows // block,),
        in_specs=[row_spec,
                  pl.BlockSpec((1, d, d_exp), lambda i, be, nv: (be[i], 0, 0)),
                  pl.BlockSpec((1, d, d_exp), lambda i, be, nv: (be[i], 0, 0)),
                  pl.BlockSpec((1, d_exp, d), lambda i, be, nv: (be[i], 0, 0))],
        out_specs=row_spec,
        scratch_shapes=[pltpu.VMEM((d, d_exp), _BF16), pltpu.VMEM((d, d_exp), _BF16),
                        pltpu.VMEM((d_exp, d), _BF16)],
    )
    return pl.pallas_call(
        _moe_ffn_kernel,
        out_shape=jax.ShapeDtypeStruct((n_rows, d), _F32),
        grid_spec=grid_spec,
        compiler_params=_params("arbitrary"),
        name="moe_ffn",
    )(block_expert, n_valid, rows, w1, w3, w2)


def _final_kernel(dest_ref, x2_ref, route_ref, gate_ref, g_ref, y_hbm, o_ref, ybuf_ref, sem_ref):
    tm = x2_ref.shape[1]

    def copy(r, k):
        return pltpu.make_async_copy(y_hbm.at[pl.ds(dest_ref[0, 0, r * TOP_K + k], 1)],
                                     ybuf_ref.at[k, pl.ds(r, 1)], sem_ref.at[0])

    _for_each_slot(tm, lambda r, k: copy(r, k).start())
    _for_each_slot(tm, lambda r, k: copy(r, k).wait())
    route = route_ref[0]
    ffn = route[:, ROUTE_GATE:ROUTE_GATE + 1] * ybuf_ref[0]
    for k in range(1, TOP_K):
        ffn = ffn + route[:, ROUTE_GATE + k:ROUTE_GATE + k + 1] * ybuf_ref[k]
    o_ref[0] = _rms(x2_ref[0] + gate_ref[0] * ffn, g_ref[...])


def _final(x2, route, dest, y_rows, gate, final_norm, *, tm):
    g, t, d = x2.shape
    mod_rows = gate.shape[1]
    mod_block = (1, 1, d) if mod_rows == 1 else (1, tm, d)
    mod_map = (lambda b, i: (b, 0, 0)) if mod_rows == 1 else (lambda b, i: (b, i, 0))
    steps = t // tm
    dest_blocks = dest.reshape(g * steps, 1, tm * TOP_K)
    tok = lambda w: pl.BlockSpec((1, tm, w), lambda b, i: (b, i, 0))
    return pl.pallas_call(
        _final_kernel,
        out_shape=jax.ShapeDtypeStruct((g, t, d), _F32),
        grid=(g, steps),
        in_specs=[pl.BlockSpec((1, 1, tm * TOP_K), lambda b, i: (b * steps + i, 0, 0), memory_space=pltpu.SMEM),
                  tok(d), tok(V7X_LANES),
                  pl.BlockSpec(mod_block, mod_map),
                  pl.BlockSpec((1, d), lambda b, i: (0, 0)),
                  pl.BlockSpec(memory_space=pl.ANY)],
        out_specs=tok(d),
        scratch_shapes=[pltpu.VMEM((TOP_K, tm, d), _F32), pltpu.SemaphoreType.DMA((1,))],
        compiler_params=_params("arbitrary", "arbitrary"),
        name="final",
    )(dest_blocks, x2, route, gate, final_norm.reshape(1, d), y_rows)


def _token_tile(t, want):
    return want if t % want == 0 else t


def kernel(x_prompt, x_sample, cache_k, cache_v, state_conv, state_h, page_table, c_prompt, c_sample,
           w_ada, b_ada, norm_mix, norm_ffn, w_in, sb_bias, norm_attn, conv_w, conv_b, rg_wa, rg_ba,
           rg_wx, rg_bx, rg_lambda, norm_rg, w_out, w_group, b_group, w_router, b_router, w1, w3, w2,
           final_norm):
    depth = w_ada.shape[0]
    assert depth == 1, "the final RMSNorm is fused into the layer's second residual add"
    bp, tp, d = x_prompt.shape
    bs, ts, _ = x_sample.shape
    n_s = bs * ts
    xp = x_prompt
    xs = x_sample.reshape(1, n_s, d)
    outs = [[] for _ in range(8)]
    for l in range(depth):
        p = dict(w_ada=w_ada[l], b_ada=b_ada[l], norm_mix=norm_mix[l], norm_ffn=norm_ffn[l], w_in=w_in[l],
                 sb_bias=sb_bias[l], norm_attn=norm_attn[l], conv_w=conv_w[l], conv_b=conv_b[l],
                 rg_wa=rg_wa[l], rg_ba=rg_ba[l], rg_wx=rg_wx[l], rg_bx=rg_bx[l], rg_lambda=rg_lambda[l],
                 norm_rg=norm_rg[l], w_out=w_out[l], w_group=w_group[l], b_group=b_group[l],
                 w_router=w_router[l], b_router=b_router[l])
        w_attn = p["w_in"].shape[1] // 5
        n_heads = w_attn // HEAD_DIM
        c_rg = w_attn
        mod = _ada(jnp.concatenate([c_prompt, c_sample], axis=0), p["w_ada"], p["b_ada"])
        mod_p = mod[:bp].reshape(bp, 6, 1, d)
        mod_s = jnp.repeat(mod[bp:].reshape(bs, 6, d), ts, axis=0).reshape(n_s, 6, d)
        shift1_p, scale1_p, gate1_p, shift2_p, scale2_p, gate2_p = (mod_p[:, j] for j in range(6))
        shift1_s, scale1_s, gate1_s, shift2_s, scale2_s, gate2_s = (mod_s[None, :, j] for j in range(6))

        w_in16 = p["w_in"].astype(_BF16)

        tm_p = _token_tile(tp, 512)
        k_p, v_p, xr_p, yb_p, q16, k16, v16 = _inproj(xp, scale1_p, shift1_p, p["norm_mix"], w_in16, tm=tm_p,
                                                      pair_major=True)
        attn_p = _attn_prompt(q16, k16, v16, p["sb_bias"], p["norm_attn"])
        conv0 = jnp.zeros((bp, CONV_WIDTH - 1, c_rg), _F32)
        h0 = jnp.zeros((bp, c_rg), _F32)
        rg_p, conv_p, h_p = _rglru(xr_p, yb_p, conv0, h0, p, tt=_token_tile(tp, 512))
        x2_p, xn2_p, route_p, counts_p = _outproj(xp, attn_p, rg_p, gate1_p, scale2_p, shift2_p, p, tm=tm_p)

        k_s, v_s, xr_s, yb_s, q_s = _inproj(xs, scale1_s, shift1_s, p["norm_mix"], w_in16, tm=n_s,
                                            pair_major=False)
        seq = lambda a: a.reshape(bs, ts, a.shape[-1])
        attn_s = _attn_sample(seq(q_s), seq(k_s), seq(v_s), cache_k[l], cache_v[l], page_table,
                              p["sb_bias"], p["norm_attn"])
        rg_s, conv_s, h_s = _rglru(seq(xr_s), seq(yb_s), state_conv[l], state_h[l], p, tt=ts)
        x2_s, xn2_s, route_s, counts_s = _outproj(xs, attn_s.reshape(1, n_s, w_attn), rg_s.reshape(1, n_s, c_rg),
                                                  gate1_s, scale2_s, shift2_s, p, tm=n_s)

        n_p = bp * tp
        n_blocks = pl.cdiv((n_p + n_s) * TOP_K + N_EXPERTS * (MOE_BLOCK - 1), MOE_BLOCK)
        slots_p = counts_p[0, :N_EXPERTS].astype(jnp.int32)
        slots_s = counts_s[0, :N_EXPERTS].astype(jnp.int32)
        first_row, block_expert, n_valid = _moe_layout(slots_p + slots_s, n_blocks, MOE_BLOCK)
        dest_p = _slot_rows(route_p.reshape(n_p, V7X_LANES), first_row)
        dest_s = _slot_rows(route_s.reshape(n_s, V7X_LANES), first_row + slots_p)
        rows = jnp.zeros((n_blocks * MOE_BLOCK, d), _F32)
        rows = _dispatch(xn2_p, dest_p, rows, tm=tm_p)
        rows = _dispatch(xn2_s, dest_s, rows, tm=n_s)
        y_rows = _moe_ffn(rows, block_expert, n_valid, w1[l], w3[l], w2[l])
        xp = _final(x2_p, route_p, dest_p, y_rows, gate2_p, final_norm, tm=tm_p)
        xs = _final(x2_s, route_s, dest_s, y_rows, gate2_s, final_norm, tm=n_s)
        hd = (n_heads, HEAD_DIM)
        for lst, val in zip(outs, (k_p.reshape(bp, tp, *hd), v_p.reshape(bp, tp, *hd), conv_p,
                                   h_p.reshape(bp, c_rg), seq(k_s).reshape(bs, ts, *hd),
                                   seq(v_s).reshape(bs, ts, *hd), conv_s, h_s.reshape(bs, c_rg))):
            lst.append(val)
    stacked = [jnp.stack(o) for o in outs]
    return (xp, xs.reshape(bs, ts, d), *stacked)
```
